```python
import math
import jax, jax.numpy as jnp
from jax import lax
import numpy as np

D_MODEL = 1024
BATCH = 8
SEQ = 4096
DEPTH = 1
DEC_BATCH = 4
DEC_SEQ = 8192
PAST_LEN = 128

N_MEM = 256
CONV_CH = 512
CONV_WIDTH = 31
DIFF_HEADS = 4
DIFF_DQ = 64
DIFF_DV = 2 * DIFF_DQ
DIFF_WIDTH = DIFF_HEADS * DIFF_DV
MIX_WIDTH = CONV_CH + DIFF_WIDTH
IN_COLS = 2 * CONV_CH + 3 * DIFF_WIDTH
MEM_HEADS = 4
MEM_HD = D_MODEL // MEM_HEADS
D_FF = 2816
FFN_CONV_WIDTH = 3
ROPE_THETA = 10000.0
Q_BLOCK = 128
EPS = 1e-6

kernel_name = "hybrid_conformer_diffattn_encoder"


def rmsnorm(x, g):
    xf = x.astype(jnp.float32)
    y = xf * lax.rsqrt(jnp.mean(xf * xf, axis=-1, keepdims=True) + EPS)
    return (y * g.astype(jnp.float32)).astype(x.dtype)


def rope(x, pos):
    d = x.shape[-1]
    inv = ROPE_THETA ** (-jnp.arange(0, d, 2, dtype=jnp.float32) / d)
    ang = pos[:, None] * inv[None, :]
    c = jnp.cos(ang)[None, :, None, :]
    s = jnp.sin(ang)[None, :, None, :]
    xf = x.astype(jnp.float32)
    x1, x2 = xf[..., : d // 2], xf[..., d // 2:]
    out = jnp.concatenate([x1 * c - x2 * s, x2 * c + x1 * s], axis=-1)
    return out.astype(x.dtype)


def dwconv(x, w, b):
    k, ch = w.shape
    p = (k - 1) // 2
    y = lax.conv_general_dilated(
        x, w.astype(x.dtype)[:, None, :], window_strides=(1,), padding=[(p, p)],
        dimension_numbers=("NWC", "WIO", "NWC"), feature_group_count=ch)
    return y + b.astype(x.dtype)


def diff_attention(q, k, v, lam, subln_g, lam_init):
    bsz, s_len = q.shape[0], q.shape[1]
    nb = s_len // Q_BLOCK
    qb = q.reshape(bsz, nb, Q_BLOCK, 2 * DIFF_HEADS, DIFF_DQ).transpose(1, 0, 2, 3, 4)
    scale = DIFF_DQ ** -0.5

    def block(qblk):
        s = jnp.einsum("bqhd,bkhd->bhqk", qblk, k).astype(jnp.float32) * scale
        p = jax.nn.softmax(s, axis=-1).reshape(bsz, DIFF_HEADS, 2, Q_BLOCK, s_len)
        a = p[:, :, 0] - lam * p[:, :, 1]
        return jnp.einsum("bhqk,bkhd->bqhd", a.astype(v.dtype), v)

    o = lax.map(block, qb)
    o = o.transpose(1, 0, 2, 3, 4).reshape(bsz, s_len, DIFF_HEADS, DIFF_DV)
    o = rmsnorm(o, subln_g) * (1.0 - lam_init)
    return o.reshape(bsz, s_len, DIFF_WIDTH)


def hybrid_mixer(h, pos, w_in, conv_dw_w, conv_dw_b, conv_norm, lq1, lk1, lq2, lk2,
                 diff_subln, w_out, lam_init):
    bsz, s_len, _ = h.shape
    z = h @ w_in
    a_val, a_gate, q, k, v = jnp.split(
        z, [CONV_CH, 2 * CONV_CH, 2 * CONV_CH + DIFF_WIDTH, 2 * CONV_CH + 2 * DIFF_WIDTH], axis=-1)
    a = a_val * jax.nn.sigmoid(a_gate)
    a = dwconv(a, conv_dw_w, conv_dw_b)
    a = jax.nn.silu(rmsnorm(a, conv_norm))
    q = rope(q.reshape(bsz, s_len, 2 * DIFF_HEADS, DIFF_DQ), pos)
    k = rope(k.reshape(bsz, s_len, 2 * DIFF_HEADS, DIFF_DQ), pos)
    v = v.reshape(bsz, s_len, DIFF_HEADS, DIFF_DV)
    lam = (jnp.exp(jnp.sum(lq1.astype(jnp.float32) * lk1.astype(jnp.float32)))
           - jnp.exp(jnp.sum(lq2.astype(jnp.float32) * lk2.astype(jnp.float32))) + lam_init)
    b = diff_attention(q, k, v, lam, diff_subln, lam_init)
    return jnp.concatenate([a, b], axis=-1) @ w_out


def memory_cross_attention(h, m, w_mq, w_mkv, w_mo):
    bsz, s_len, _ = h.shape
    n = m.shape[1]
    q = (h @ w_mq).reshape(bsz, s_len, MEM_HEADS, MEM_HD)
    k, v = jnp.split(m @ w_mkv, 2, axis=-1)
    k = k.reshape(bsz, n, MEM_HEADS, MEM_HD)
    v = v.reshape(bsz, n, MEM_HEADS, MEM_HD)
    s = jnp.einsum("bqhd,bkhd->bhqk", q, k).astype(jnp.float32) * (MEM_HD ** -0.5)
    p = jax.nn.softmax(s, axis=-1).astype(v.dtype)
    o = jnp.einsum("bhqk,bkhd->bqhd", p, v).reshape(bsz, s_len, D_MODEL)
    return o @ w_mo


def conv_ffn(h, w_up, ffn_dw_w, ffn_dw_b, w_down):
    u = dwconv(h @ w_up, ffn_dw_w, ffn_dw_b)
    val, gate = jnp.split(u, 2, axis=-1)
    return (jax.nn.silu(gate) * val) @ w_down


def encoder(x, mem, norm_mix, w_in, conv_dw_w, conv_dw_b, conv_norm, lambda_q1, lambda_k1,
            lambda_q2, lambda_k2, diff_subln, w_out, norm_cross, norm_mem, w_mq, w_mkv, w_mo,
            norm_ffn, w_up, ffn_dw_w, ffn_dw_b, w_down, norm_final):
    pos = jnp.arange(x.shape[1], dtype=jnp.float32)
    for l in range(DEPTH):
        lam_init = 0.8 - 0.6 * math.exp(-0.3 * l)
        x = x + hybrid_mixer(rmsnorm(x, norm_mix[l]), pos, w_in[l], conv_dw_w[l], conv_dw_b[l],
                             conv_norm[l], lambda_q1[l], lambda_k1[l], lambda_q2[l], lambda_k2[l],
                             diff_subln[l], w_out[l], lam_init)
        x = x + memory_cross_attention(rmsnorm(x, norm_cross[l]), rmsnorm(mem, norm_mem[l]),
                                       w_mq[l], w_mkv[l], w_mo[l])
        x = x + conv_ffn(rmsnorm(x, norm_ffn[l]), w_up[l], ffn_dw_w[l], ffn_dw_b[l], w_down[l])
    return rmsnorm(x, norm_final)


def setup_inputs(seed: int = 0) -> dict:
    key = jax.random.key(seed)
    ks = jax.random.split(key, 32)
    f32 = jnp.float32

    def nrm(k, shape, scale):
        return jax.random.normal(k, shape, f32) * scale

    def gain(k, shape):
        return 1.0 + 0.05 * jax.random.normal(k, shape, f32)

    L = DEPTH
    return {
        "x_prompt": nrm(ks[0], (BATCH, SEQ, D_MODEL), 1.0),
        "x_sample": nrm(ks[1], (DEC_BATCH, DEC_SEQ, D_MODEL), 1.0),
        "mem_prompt": nrm(ks[2], (BATCH, N_MEM, D_MODEL), 1.0),
        "mem_sample": nrm(ks[3], (DEC_BATCH, N_MEM, D_MODEL), 1.0),
        "norm_mix": gain(ks[4], (L, D_MODEL)),
        "w_in": nrm(ks[5], (L, D_MODEL, IN_COLS), D_MODEL ** -0.5),
        "conv_dw_w": nrm(ks[6], (L, CONV_WIDTH, CONV_CH), CONV_WIDTH ** -0.5),
        "conv_dw_b": nrm(ks[7], (L, CONV_CH), 0.02),
        "conv_norm": gain(ks[8], (L, CONV_CH)),
        "lambda_q1": nrm(ks[9], (L, DIFF_DQ), 0.1),
        "lambda_k1": nrm(ks[10], (L, DIFF_DQ), 0.1),
        "lambda_q2": nrm(ks[11], (L, DIFF_DQ), 0.1),
        "lambda_k2": nrm(ks[12], (L, DIFF_DQ), 0.1),
        "diff_subln": gain(ks[13], (L, DIFF_DV)),
        "w_out": nrm(ks[14], (L, MIX_WIDTH, D_MODEL), MIX_WIDTH ** -0.5),
        "norm_cross": gain(ks[15], (L, D_MODEL)),
        "norm_mem": gain(ks[16], (L, D_MODEL)),
        "w_mq": nrm(ks[17], (L, D_MODEL, D_MODEL), D_MODEL ** -0.5),
        "w_mkv": nrm(ks[18], (L, D_MODEL, 2 * D_MODEL), D_MODEL ** -0.5),
        "w_mo": nrm(ks[19], (L, D_MODEL, D_MODEL), D_MODEL ** -0.5),
        "norm_ffn": gain(ks[20], (L, D_MODEL)),
        "w_up": nrm(ks[21], (L, D_MODEL, 2 * D_FF), D_MODEL ** -0.5),
        "ffn_dw_w": nrm(ks[22], (L, FFN_CONV_WIDTH, 2 * D_FF), FFN_CONV_WIDTH ** -0.5),
        "ffn_dw_b": nrm(ks[23], (L, 2 * D_FF), 0.02),
        "w_down": nrm(ks[24], (L, D_FF, D_MODEL), D_FF ** -0.5),
        "norm_final": gain(ks[25], (D_MODEL,)),
    }


def reference(x_prompt, x_sample, mem_prompt, mem_sample, norm_mix, w_in, conv_dw_w, conv_dw_b,
              conv_norm, lambda_q1, lambda_k1, lambda_q2, lambda_k2, diff_subln, w_out,
              norm_cross, norm_mem, w_mq, w_mkv, w_mo, norm_ffn, w_up, ffn_dw_w, ffn_dw_b,
              w_down, norm_final):
    y_prompt = encoder(x_prompt, mem_prompt, norm_mix, w_in, conv_dw_w, conv_dw_b, conv_norm,
                       lambda_q1, lambda_k1, lambda_q2, lambda_k2, diff_subln, w_out,
                       norm_cross, norm_mem, w_mq, w_mkv, w_mo, norm_ffn, w_up, ffn_dw_w,
                       ffn_dw_b, w_down, norm_final)
    y_sample = encoder(x_sample, mem_sample, norm_mix, w_in, conv_dw_w, conv_dw_b, conv_norm,
                       lambda_q1, lambda_k1, lambda_q2, lambda_k2, diff_subln, w_out,
                       norm_cross, norm_mem, w_mq, w_mkv, w_mo, norm_ffn, w_up, ffn_dw_w,
                       ffn_dw_b, w_down, norm_final)
    return (y_prompt, y_sample)
```

```python
import functools
import math

import jax
import jax.numpy as jnp
from jax import lax
from jax.experimental import pallas as pl
from jax.experimental.pallas import tpu as pltpu

F32 = jnp.float32
BF16 = jnp.bfloat16

D_MODEL = 1024
N_MEM = 256
CONV_CH = 512
CONV_WIDTH = 31
DIFF_HEADS = 4
DIFF_DQ = 64
DIFF_DV = 128
DIFF_WIDTH = DIFF_HEADS * DIFF_DV
IN_COLS = 2 * CONV_CH + 3 * DIFF_WIDTH
MEM_HEADS = 4
MEM_HD = D_MODEL // MEM_HEADS
D_FF = 2816
ROPE_THETA = 10000.0
EPS = 1e-6
LAM_INIT = 0.8 - 0.6 * math.exp(-0.3 * 0)

TM = 512
TQ = 512
TK = 512
V_ROWS = DIFF_DV + 16
HALO = 16
CONV_ROWS = 32
FF_CHUNK = 256
VMEM_LIMIT = 56 * 1024 * 1024
NEG_BIG = -1e30


def _rms(x, g):
    ms = jnp.mean(x * x, axis=-1, keepdims=True)
    return x * lax.rsqrt(ms + EPS) * g


def _params(n_axes):
    return pltpu.CompilerParams(dimension_semantics=("arbitrary",) * n_axes,
                                vmem_limit_bytes=VMEM_LIMIT)


def _const_spec(shape):
    return pl.BlockSpec(shape, lambda *_: (0,) * len(shape), pipeline_mode=pl.Buffered(1))


def _inproj_kernel(x_ref, g_ref, w_ref, cos_ref, sin_ref, glu_ref, qt_ref, k_ref, vt_ref):
    h = _rms(x_ref[0], g_ref[...]).astype(BF16)
    z = jnp.dot(h, w_ref[...], preferred_element_type=F32)
    glu_ref[0] = z[:, :CONV_CH] * jax.nn.sigmoid(z[:, CONV_CH:2 * CONV_CH])

    cos = cos_ref[...]
    sin = sin_ref[...]
    lane = lax.broadcasted_iota(jnp.int32, (TM, 128), 1)
    first_half = (lane % DIFF_DQ) < (DIFF_DQ // 2)

    def rope(t):
        partner = jnp.where(first_half, pltpu.roll(t, 96, 1), pltpu.roll(t, 32, 1))
        return t * cos + partner * sin

    q_off = 2 * CONV_CH
    k_off = q_off + DIFF_WIDTH
    v_off = k_off + DIFF_WIDTH
    for j in range(DIFF_HEADS):
        q = rope(z[:, q_off + 128 * j:q_off + 128 * (j + 1)]) * (DIFF_DQ ** -0.5)
        qt_ref[0, 128 * j:128 * (j + 1), :] = q.T.astype(BF16)
        k = rope(z[:, k_off + 128 * j:k_off + 128 * (j + 1)])
        k_ref[0, :, 128 * j:128 * (j + 1)] = k.astype(BF16)
        v = z[:, v_off + 128 * j:v_off + 128 * (j + 1)]
        vt_ref[0, j, 0, 0:DIFF_DV, :] = v.T.astype(BF16)
        pad_row = lax.broadcasted_iota(jnp.int32, (V_ROWS - DIFF_DV, TM), 0)
        vt_ref[0, j, 0, DIFF_DV:V_ROWS, :] = jnp.where(pad_row == 0, 1.0, 0.0).astype(BF16)


def _inproj(x, g, w_in, cos, sin):
    b, s, _ = x.shape
    nt = s // TM
    return pl.pallas_call(
        _inproj_kernel,
        grid=(b, nt),
        in_specs=[
            pl.BlockSpec((1, TM, D_MODEL), lambda bi, i: (bi, i, 0)),
            _const_spec((1, D_MODEL)),
            _const_spec((D_MODEL, IN_COLS)),
            pl.BlockSpec((TM, 128), lambda bi, i: (i, 0)),
            pl.BlockSpec((TM, 128), lambda bi, i: (i, 0)),
        ],
        out_specs=[
            pl.BlockSpec((1, TM, CONV_CH), lambda bi, i: (bi, i, 0)),
            pl.BlockSpec((1, DIFF_WIDTH, TM), lambda bi, i: (bi, 0, i)),
            pl.BlockSpec((1, TM, DIFF_WIDTH), lambda bi, i: (bi, i, 0)),
            pl.BlockSpec((1, DIFF_HEADS, 1, V_ROWS, TM), lambda bi, i: (bi, 0, i, 0, 0)),
        ],
        out_shape=[
            jax.ShapeDtypeStruct((b, s, CONV_CH), F32),
            jax.ShapeDtypeStruct((b, DIFF_WIDTH, s), BF16),
            jax.ShapeDtypeStruct((b, s, DIFF_WIDTH), BF16),
            jax.ShapeDtypeStruct((b, DIFF_HEADS, nt, V_ROWS, TM), BF16),
        ],
        compiler_params=_params(2),
        name="inproj",
    )(x, g, w_in, cos, sin)


def _conv_kernel(main_ref, prev_ref, next_ref, w_ref, b_ref, g_ref, o_ref, win_ref):
    i = pl.program_id(1)
    n = pl.num_programs(1)
    prev = prev_ref[0]
    nxt = next_ref[0]
    win_ref[0:HALO, :] = jnp.where(i > 0, prev, jnp.zeros_like(prev))
    win_ref[HALO:HALO + TM, :] = main_ref[0]
    win_ref[HALO + TM:2 * HALO + TM, :] = jnp.where(i < n - 1, nxt, jnp.zeros_like(nxt))
    pad = (CONV_WIDTH - 1) // 2
    for r in range(TM // CONV_ROWS):
        base = HALO + r * CONV_ROWS - pad
        acc = jnp.broadcast_to(b_ref[...], (CONV_ROWS, CONV_CH))
        for j in range(CONV_WIDTH):
            acc = acc + win_ref[base + j:base + j + CONV_ROWS, :] * w_ref[j:j + 1, :]
        y = _rms(acc, g_ref[...])
        o_ref[0, r * CONV_ROWS:(r + 1) * CONV_ROWS, :] = (y * jax.nn.sigmoid(y)).astype(BF16)


def _halo_specs(width, seq_len):
    per_tile = TM // HALO
    last = seq_len // HALO - 1

    def prev_map(bi, i):
        return (bi, jnp.maximum(i * per_tile - 1, 0), 0)

    def next_map(bi, i):
        return (bi, jnp.minimum((i + 1) * per_tile, last), 0)

    return (pl.BlockSpec((1, HALO, width), prev_map), pl.BlockSpec((1, HALO, width), next_map))


def _conv_mixer(glu, w, bias, g):
    b, s, _ = glu.shape
    prev_spec, next_spec = _halo_specs(CONV_CH, s)
    return pl.pallas_call(
        _conv_kernel,
        grid=(b, s // TM),
        in_specs=[
            pl.BlockSpec((1, TM, CONV_CH), lambda bi, i: (bi, i, 0)),
            prev_spec,
            next_spec,
            _const_spec((CONV_WIDTH + 1, CONV_CH)),
            _const_spec((1, CONV_CH)),
            _const_spec((1, CONV_CH)),
        ],
        out_specs=pl.BlockSpec((1, TM, CONV_CH), lambda bi, i: (bi, i, 0)),
        out_shape=jax.ShapeDtypeStruct((b, s, CONV_CH), BF16),
        scratch_shapes=[pltpu.VMEM((TM + 2 * HALO, CONV_CH), F32)],
        compiler_params=_params(2),
        name="conv_mixer",
    )(glu, glu, glu, w, bias, g)


def _attn_kernel(qt_ref, k_ref, vt_ref, lq1_ref, lk1_ref, lq2_ref, lk2_ref, g_ref, o_ref, acc_ref,
                 *, n_chunks):
    qt = qt_ref[0]
    zeros = jnp.zeros((DIFF_DQ, TQ), BF16)
    qq = jnp.concatenate(
        [jnp.concatenate([qt[:DIFF_DQ], zeros], axis=0),
         jnp.concatenate([zeros, qt[DIFF_DQ:]], axis=0)], axis=1)
    acc_ref[...] = jnp.zeros_like(acc_ref)

    def body(c, m):
        off = pl.multiple_of(c * TK, TK)
        s = jnp.dot(k_ref[0, pl.ds(off, TK), :], qq, preferred_element_type=F32)
        m_new = jnp.maximum(m, jnp.max(s, axis=0, keepdims=True))
        alpha = jnp.exp(m - m_new)
        p = jnp.exp(s - m_new).astype(BF16)
        pv = jnp.dot(vt_ref[0, 0, c], p, preferred_element_type=F32)
        acc_ref[...] = acc_ref[...] * alpha + pv
        return m_new

    lax.fori_loop(0, n_chunks, body, jnp.full((1, 2 * TQ), NEG_BIG, F32))

    acc = acc_ref[...]
    o = acc[:DIFF_DV, :] * (1.0 / acc[DIFF_DV:DIFF_DV + 1, :])
    lam = (jnp.exp(jnp.sum(lq1_ref[...] * lk1_ref[...], axis=-1, keepdims=True))
           - jnp.exp(jnp.sum(lq2_ref[...] * lk2_ref[...], axis=-1, keepdims=True)) + LAM_INIT)
    d = o[:, :TQ] - lam * o[:, TQ:]
    ms = jnp.mean(d * d, axis=0, keepdims=True)
    y = d * lax.rsqrt(ms + EPS) * g_ref[...] * (1.0 - LAM_INIT)
    o_ref[0] = y.T.astype(BF16)


def _diff_attention(qt, k, vt, lq1, lk1, lq2, lk2, g_col):
    b, _, s = qt.shape
    n_chunks = s // TK
    return pl.pallas_call(
        functools.partial(_attn_kernel, n_chunks=n_chunks),
        grid=(b, DIFF_HEADS, s // TQ),
        in_specs=[
            pl.BlockSpec((1, 2 * DIFF_DQ, TQ), lambda bi, h, qi: (bi, h, qi)),
            pl.BlockSpec((1, s, 2 * DIFF_DQ), lambda bi, h, qi: (bi, 0, h)),
            pl.BlockSpec((1, 1, n_chunks, V_ROWS, TK), lambda bi, h, qi: (bi, h, 0, 0, 0)),
            _const_spec((1, DIFF_DQ)),
            _const_spec((1, DIFF_DQ)),
            _const_spec((1, DIFF_DQ)),
            _const_spec((1, DIFF_DQ)),
            _const_spec((DIFF_DV, 1)),
        ],
        out_specs=pl.BlockSpec((1, TQ, DIFF_DV), lambda bi, h, qi: (bi, qi, h)),
        out_shape=jax.ShapeDtypeStruct((b, s, DIFF_WIDTH), BF16),
        scratch_shapes=[pltpu.VMEM((V_ROWS, 2 * TQ), F32)],
        compiler_params=_params(3),
        name="diff_attention",
    )(qt, k, vt, lq1, lk1, lq2, lk2, g_col)


def _memkv_kernel(m_ref, g_ref, w_ref, k_ref, v_ref):
    h = _rms(m_ref[0], g_ref[...]).astype(BF16)
    kv = jnp.dot(h, w_ref[...], preferred_element_type=F32)
    k_ref[0] = kv[:, :D_MODEL].astype(BF16)
    v_ref[0] = kv[:, D_MODEL:].astype(BF16)


def _mem_kv(mem, g, w_mkv):
    b = mem.shape[0]
    blk = pl.BlockSpec((1, N_MEM, D_MODEL), lambda bi: (bi, 0, 0))
    return pl.pallas_call(
        _memkv_kernel,
        grid=(b,),
        in_specs=[blk, _const_spec((1, D_MODEL)), _const_spec((D_MODEL, 2 * D_MODEL))],
        out_specs=[blk, blk],
        out_shape=[jax.ShapeDtypeStruct((b, N_MEM, D_MODEL), BF16)] * 2,
        compiler_params=_params(1),
        name="mem_kv",
    )(mem, g, w_mkv)


def _cross_kernel(x_ref, a_ref, b_ref, wout_ref, gc_ref, wq_ref, km_ref, vm_ref, wo_ref, gf_ref,
                  x2_ref, hf_ref):
    x1 = (x_ref[0]
          + jnp.dot(a_ref[0], wout_ref[:CONV_CH, :], preferred_element_type=F32)
          + jnp.dot(b_ref[0], wout_ref[CONV_CH:, :], preferred_element_type=F32))
    hq = _rms(x1, gc_ref[...]).astype(BF16)
    q = (jnp.dot(hq, wq_ref[...], preferred_element_type=F32) * (MEM_HD ** -0.5)).astype(BF16)
    heads = []
    for h in range(MEM_HEADS):
        sl = slice(h * MEM_HD, (h + 1) * MEM_HD)
        s = lax.dot_general(q[:, sl], km_ref[0, :, sl], (((1,), (1,)), ((), ())),
                            preferred_element_type=F32)
        p = jnp.exp(s - jnp.max(s, axis=-1, keepdims=True))
        l = jnp.sum(p, axis=-1, keepdims=True)
        o = jnp.dot(p.astype(BF16), vm_ref[0, :, sl], preferred_element_type=F32)
        heads.append((o * (1.0 / l)).astype(BF16))
    o_all = jnp.concatenate(heads, axis=-1)
    x2 = x1 + jnp.dot(o_all, wo_ref[...], preferred_element_type=F32)
    x2_ref[0] = x2
    hf_ref[0] = _rms(x2, gf_ref[...]).astype(BF16)


def _cross(x, a, bb, w_out, g_cross, w_mq, k_mem, v_mem, w_mo, g_ffn):
    b, s, _ = x.shape
    tile = lambda width: pl.BlockSpec((1, TM, width), lambda bi, i: (bi, i, 0))
    mem_blk = pl.BlockSpec((1, N_MEM, D_MODEL), lambda bi, i: (bi, 0, 0))
    return pl.pallas_call(
        _cross_kernel,
        grid=(b, s // TM),
        in_specs=[
            tile(D_MODEL), tile(CONV_CH), tile(DIFF_WIDTH),
            _const_spec((D_MODEL, D_MODEL)), _const_spec((1, D_MODEL)),
            _const_spec((D_MODEL, D_MODEL)), mem_blk, mem_blk,
            _const_spec((D_MODEL, D_MODEL)), _const_spec((1, D_MODEL)),
        ],
        out_specs=[tile(D_MODEL), tile(D_MODEL)],
        out_shape=[jax.ShapeDtypeStruct((b, s, D_MODEL), F32),
                   jax.ShapeDtypeStruct((b, s, D_MODEL), BF16)],
        compiler_params=_params(2),
        name="cross_attention",
    )(x, a, bb, w_out, g_cross, w_mq, k_mem, v_mem, w_mo, g_ffn)


def _ffn_kernel(h_ref, hprev_ref, hnext_ref, x2_ref, wup_ref, dw_ref, db_ref, wdown_ref, gfin_ref,
                y_ref, u_ref, gate_ref):
    i = pl.program_id(1)
    n = pl.num_programs(1)
    prev = hprev_ref[0]
    nxt = hnext_ref[0]
    hwin = jnp.concatenate(
        [jnp.where(i > 0, prev, jnp.zeros_like(prev)), h_ref[0],
         jnp.where(i < n - 1, nxt, jnp.zeros_like(nxt))], axis=0)

    def conv3(col0):
        cols = slice(col0, col0 + FF_CHUNK)
        u_ref[...] = jnp.dot(hwin, wup_ref[:, cols], preferred_element_type=F32)
        return (u_ref[HALO - 1:HALO - 1 + TM, :] * dw_ref[0:1, cols]
                + u_ref[HALO:HALO + TM, :] * dw_ref[1:2, cols]
                + u_ref[HALO + 1:HALO + 1 + TM, :] * dw_ref[2:3, cols]
                + db_ref[:, cols])

    for c in range(D_FF // FF_CHUNK):
        val = conv3(c * FF_CHUNK)
        gate = conv3(D_FF + c * FF_CHUNK)
        gate_ref[:, c * FF_CHUNK:(c + 1) * FF_CHUNK] = (gate * jax.nn.sigmoid(gate) * val).astype(BF16)
    x3 = x2_ref[0] + jnp.dot(gate_ref[...], wdown_ref[...], preferred_element_type=F32)
    y_ref[0] = _rms(x3, gfin_ref[...])


def _ffn(hf, x2, w_up, dw_w, dw_b, w_down, g_final):
    b, s, _ = x2.shape
    tile = pl.BlockSpec((1, TM, D_MODEL), lambda bi, i: (bi, i, 0))
    prev_spec, next_spec = _halo_specs(D_MODEL, s)
    return pl.pallas_call(
        _ffn_kernel,
        grid=(b, s // TM),
        in_specs=[
            tile, prev_spec, next_spec, tile,
            _const_spec((D_MODEL, 2 * D_FF)), _const_spec((8, 2 * D_FF)), _const_spec((1, 2 * D_FF)),
            _const_spec((D_FF, D_MODEL)), _const_spec((1, D_MODEL)),
        ],
        out_specs=tile,
        out_shape=jax.ShapeDtypeStruct((b, s, D_MODEL), F32),
        scratch_shapes=[pltpu.VMEM((TM + 2 * HALO, FF_CHUNK), F32), pltpu.VMEM((TM, D_FF), BF16)],
        compiler_params=_params(2),
        name="conv_ffn",
    )(hf, hf, hf, x2, w_up, dw_w, dw_b, w_down, g_final)


def _rope_tables(s):
    inv = ROPE_THETA ** (-jnp.arange(0, DIFF_DQ, 2, dtype=F32) / DIFF_DQ)
    ang = jnp.arange(s, dtype=F32)[:, None] * inv[None, :]
    c, sn = jnp.cos(ang), jnp.sin(ang)
    return jnp.concatenate([c, c, c, c], axis=1), jnp.concatenate([-sn, sn, -sn, sn], axis=1)


def _encoder(x, mem, p):
    cos, sin = _rope_tables(x.shape[1])
    glu, qt, k, vt = _inproj(x, p["norm_mix"], p["w_in"], cos, sin)
    a = _conv_mixer(glu, p["conv_dw_w"], p["conv_dw_b"], p["conv_norm"])
    bb = _diff_attention(qt, k, vt, p["lq1"], p["lk1"], p["lq2"], p["lk2"], p["diff_subln"])
    k_mem, v_mem = _mem_kv(mem, p["norm_mem"], p["w_mkv"])
    x2, hf = _cross(x, a, bb, p["w_out"], p["norm_cross"], p["w_mq"], k_mem, v_mem, p["w_mo"],
                    p["norm_ffn"])
    return _ffn(hf, x2, p["w_up"], p["ffn_dw_w"], p["ffn_dw_b"], p["w_down"], p["norm_final"])


def kernel(x_prompt, x_sample, mem_prompt, mem_sample, norm_mix, w_in, conv_dw_w, conv_dw_b, conv_norm, lambda_q1, lambda_k1, lambda_q2, lambda_k2, diff_subln, w_out, norm_cross, norm_mem, w_mq, w_mkv, w_mo, norm_ffn, w_up, ffn_dw_w, ffn_dw_b, w_down, norm_final):
    row = lambda v: v.reshape(1, -1).astype(F32)
    p = {
        "norm_mix": row(norm_mix[0]),
        "w_in": w_in[0].astype(BF16),
        "conv_dw_w": jnp.pad(conv_dw_w[0], ((0, 1), (0, 0))),
        "conv_dw_b": row(conv_dw_b[0]),
        "conv_norm": row(conv_norm[0]),
        "lq1": row(lambda_q1[0]), "lk1": row(lambda_k1[0]),
        "lq2": row(lambda_q2[0]), "lk2": row(lambda_k2[0]),
        "diff_subln": diff_subln[0].reshape(-1, 1).astype(F32),
        "w_out": w_out[0].astype(BF16),
        "norm_cross": row(norm_cross[0]),
        "norm_mem": row(norm_mem[0]),
        "w_mq": w_mq[0].astype(BF16),
        "w_mkv": w_mkv[0].astype(BF16),
        "w_mo": w_mo[0].astype(BF16),
        "norm_ffn": row(norm_ffn[0]),
        "w_up": w_up[0].astype(BF16),
        "ffn_dw_w": jnp.pad(ffn_dw_w[0], ((0, 5), (0, 0))),
        "ffn_dw_b": row(ffn_dw_b[0]),
        "w_down": w_down[0].astype(BF16),
        "norm_final": row(norm_final),
    }
    return (_encoder(x_prompt, mem_prompt, p), _encoder(x_sample, mem_sample, p))
```

```python
import functools
import math

import jax
import jax.numpy as jnp
from jax import lax
from jax.experimental import pallas as pl
from jax.experimental.pallas import tpu as pltpu

F32 = jnp.float32
BF16 = jnp.bfloat16

D_MODEL = 1024
N_MEM = 256
CONV_CH = 512
CONV_WIDTH = 31
DIFF_HEADS = 4
DIFF_DQ = 64
DIFF_DV = 128
DIFF_WIDTH = DIFF_HEADS * DIFF_DV
IN_COLS = 2 * CONV_CH + 3 * DIFF_WIDTH
MEM_HEADS = 4
MEM_HD = D_MODEL // MEM_HEADS
D_FF = 2816
ROPE_THETA = 10000.0
EPS = 1e-6
LAM_INIT = 0.8 - 0.6 * math.exp(-0.3 * 0)
Q_SCALE = DIFF_DQ ** -0.5 * math.log2(math.e)

TM = 512
TQ = 512
TK = 512
V_ROWS = DIFF_DV + 16
HALO = 16
CONV_ROWS = 32
FF_CHUNK = 256
VMEM_LIMIT = 56 * 1024 * 1024
NEG_BIG = -1e30


def _rms(x, g):
    ms = jnp.mean(x * x, axis=-1, keepdims=True)
    return x * lax.rsqrt(ms + EPS) * g


def _params(n_axes):
    return pltpu.CompilerParams(dimension_semantics=("arbitrary",) * n_axes,
                                vmem_limit_bytes=VMEM_LIMIT)


def _const_spec(shape):
    return pl.BlockSpec(shape, lambda *_: (0,) * len(shape), pipeline_mode=pl.Buffered(1))


def _inproj_kernel(x_ref, g_ref, w_ref, cos_ref, sin_ref, glu_ref, qt_ref, k_ref, vt_ref):
    h = _rms(x_ref[0], g_ref[...]).astype(BF16)
    z = jnp.dot(h, w_ref[...], preferred_element_type=F32)
    glu_ref[0] = z[:, :CONV_CH] * jax.nn.sigmoid(z[:, CONV_CH:2 * CONV_CH])

    cos = cos_ref[...]
    sin = sin_ref[...]
    lane = lax.broadcasted_iota(jnp.int32, (TM, 128), 1)
    first_half = (lane % DIFF_DQ) < (DIFF_DQ // 2)

    def rope(t):
        partner = jnp.where(first_half, pltpu.roll(t, 96, 1), pltpu.roll(t, 32, 1))
        return t * cos + partner * sin

    q_off = 2 * CONV_CH
    k_off = q_off + DIFF_WIDTH
    v_off = k_off + DIFF_WIDTH
    for j in range(DIFF_HEADS):
        q = rope(z[:, q_off + 128 * j:q_off + 128 * (j + 1)]) * Q_SCALE
        qt_ref[0, j, 0] = q.T.astype(BF16)
        k = rope(z[:, k_off + 128 * j:k_off + 128 * (j + 1)])
        k_ref[0, :, 128 * j:128 * (j + 1)] = k.astype(BF16)
        v = z[:, v_off + 128 * j:v_off + 128 * (j + 1)]
        vt_ref[0, j, 0, 0:DIFF_DV, :] = v.T.astype(BF16)
        pad_row = lax.broadcasted_iota(jnp.int32, (V_ROWS - DIFF_DV, TM), 0)
        vt_ref[0, j, 0, DIFF_DV:V_ROWS, :] = jnp.where(pad_row == 0, 1.0, 0.0).astype(BF16)


def _inproj(x, g, w_in, cos, sin):
    b, s, _ = x.shape
    nt = s // TM
    return pl.pallas_call(
        _inproj_kernel,
        grid=(b, nt),
        in_specs=[
            pl.BlockSpec((1, TM, D_MODEL), lambda bi, i: (bi, i, 0)),
            _const_spec((1, D_MODEL)),
            _const_spec((D_MODEL, IN_COLS)),
            pl.BlockSpec((TM, 128), lambda bi, i: (i, 0)),
            pl.BlockSpec((TM, 128), lambda bi, i: (i, 0)),
        ],
        out_specs=[
            pl.BlockSpec((1, TM, CONV_CH), lambda bi, i: (bi, i, 0)),
            pl.BlockSpec((1, DIFF_HEADS, 1, 2 * DIFF_DQ, TM), lambda bi, i: (bi, 0, i, 0, 0)),
            pl.BlockSpec((1, TM, DIFF_WIDTH), lambda bi, i: (bi, i, 0)),
            pl.BlockSpec((1, DIFF_HEADS, 1, V_ROWS, TM), lambda bi, i: (bi, 0, i, 0, 0)),
        ],
        out_shape=[
            jax.ShapeDtypeStruct((b, s, CONV_CH), F32),
            jax.ShapeDtypeStruct((b, DIFF_HEADS, nt, 2 * DIFF_DQ, TM), BF16),
            jax.ShapeDtypeStruct((b, s, DIFF_WIDTH), BF16),
            jax.ShapeDtypeStruct((b, DIFF_HEADS, nt, V_ROWS, TM), BF16),
        ],
        compiler_params=_params(2),
        name="inproj",
    )(x, g, w_in, cos, sin)


def _conv_kernel(main_ref, prev_ref, next_ref, w_ref, b_ref, g_ref, o_ref, win_ref):
    i = pl.program_id(1)
    n = pl.num_programs(1)
    prev = prev_ref[0]
    nxt = next_ref[0]
    win_ref[0:HALO, :] = jnp.where(i > 0, prev, jnp.zeros_like(prev))
    win_ref[HALO:HALO + TM, :] = main_ref[0]
    win_ref[HALO + TM:2 * HALO + TM, :] = jnp.where(i < n - 1, nxt, jnp.zeros_like(nxt))
    pad = (CONV_WIDTH - 1) // 2
    for r in range(TM // CONV_ROWS):
        base = HALO + r * CONV_ROWS - pad
        acc = jnp.broadcast_to(b_ref[...], (CONV_ROWS, CONV_CH))
        for j in range(CONV_WIDTH):
            acc = acc + win_ref[base + j:base + j + CONV_ROWS, :] * w_ref[j:j + 1, :]
        y = _rms(acc, g_ref[...])
        o_ref[0, r * CONV_ROWS:(r + 1) * CONV_ROWS, :] = (y * jax.nn.sigmoid(y)).astype(BF16)


def _halo_specs(width, seq_len):
    per_tile = TM // HALO
    last = seq_len // HALO - 1

    def prev_map(bi, i):
        return (bi, jnp.maximum(i * per_tile - 1, 0), 0)

    def next_map(bi, i):
        return (bi, jnp.minimum((i + 1) * per_tile, last), 0)

    return (pl.BlockSpec((1, HALO, width), prev_map), pl.BlockSpec((1, HALO, width), next_map))


def _conv_mixer(glu, w, bias, g):
    b, s, _ = glu.shape
    prev_spec, next_spec = _halo_specs(CONV_CH, s)
    return pl.pallas_call(
        _conv_kernel,
        grid=(b, s // TM),
        in_specs=[
            pl.BlockSpec((1, TM, CONV_CH), lambda bi, i: (bi, i, 0)),
            prev_spec,
            next_spec,
            _const_spec((CONV_WIDTH + 1, CONV_CH)),
            _const_spec((1, CONV_CH)),
            _const_spec((1, CONV_CH)),
        ],
        out_specs=pl.BlockSpec((1, TM, CONV_CH), lambda bi, i: (bi, i, 0)),
        out_shape=jax.ShapeDtypeStruct((b, s, CONV_CH), BF16),
        scratch_shapes=[pltpu.VMEM((TM + 2 * HALO, CONV_CH), F32)],
        compiler_params=_params(2),
        name="conv_mixer",
    )(glu, glu, glu, w, bias, g)


def _attn_kernel(qt_ref, k_ref, vt_ref, lq1_ref, lk1_ref, lq2_ref, lk2_ref, g_ref, o_ref,
                 s_ref, p_ref, acc_ref, *, n_chunks, n_qtiles):
    n_items = n_chunks * n_qtiles
    lg = n_chunks.bit_length() - 1
    zeros = jnp.zeros((DIFF_DQ, TQ), BF16)
    lam = (jnp.exp(jnp.sum(lq1_ref[...] * lk1_ref[...], axis=-1, keepdims=True))
           - jnp.exp(jnp.sum(lq2_ref[...] * lk2_ref[...], axis=-1, keepdims=True)) + LAM_INIT)

    def stage1(t, m_prev):
        qi = t >> lg
        c = t & (n_chunks - 1)
        qt = qt_ref[0, 0, qi]
        qq = jnp.concatenate(
            [jnp.concatenate([qt[:DIFF_DQ], zeros], axis=0),
             jnp.concatenate([zeros, qt[DIFF_DQ:]], axis=0)], axis=1)
        off = pl.multiple_of(c * TK, TK)
        s = jnp.dot(k_ref[0, pl.ds(off, TK), :], qq, preferred_element_type=F32)
        s_ref[...] = s
        m_prev = jnp.where(c == 0, NEG_BIG, m_prev)
        m_new = jnp.maximum(m_prev, jnp.max(s, axis=0, keepdims=True))
        return m_new, jnp.exp2(m_prev - m_new)

    def stage2(m):
        p_ref[...] = jnp.exp2(s_ref[...] - m).astype(BF16)

    def stage3(t, a):
        c = t & (n_chunks - 1)
        pv = jnp.dot(vt_ref[0, 0, c], p_ref[...], preferred_element_type=F32)
        acc_ref[...] = acc_ref[...] * a + pv

    def finalize(qi):
        acc = acc_ref[...]
        o = acc[:DIFF_DV, :] * (1.0 / acc[DIFF_DV:DIFF_DV + 1, :])
        d = o[:, :TQ] - lam * o[:, TQ:]
        ms = jnp.mean(d * d, axis=0, keepdims=True)
        y = d * lax.rsqrt(ms + EPS) * g_ref[...] * (1.0 - LAM_INIT)
        o_ref[0, pl.ds(pl.multiple_of(qi * TQ, TQ), TQ), :] = y.T.astype(BF16)

    acc_ref[...] = jnp.zeros_like(acc_ref)
    m0, a0 = stage1(0, jnp.full((1, 2 * TQ), NEG_BIG, F32))
    stage2(m0)
    m1, a1 = stage1(1, m0)

    def step(t, carry, may_finalize):
        m_prev, a_prev1, a_prev2 = carry
        stage3(t - 2, a_prev2)
        stage2(m_prev)
        m_t, a_t = stage1(t, m_prev)
        if may_finalize:
            @pl.when(((t - 2) & (n_chunks - 1)) == n_chunks - 1)
            def _():
                finalize((t - 2) >> lg)

        return m_t, a_t, a_prev1

    def pair(j, carry):
        t = 2 * j + 2
        return step(t + 1, step(t, carry, False), True)

    m_last, a_last1, a_last2 = lax.fori_loop(0, (n_items - 2) // 2, pair, (m1, a1, a0))
    stage3(n_items - 2, a_last2)
    stage2(m_last)
    stage3(n_items - 1, a_last1)
    finalize(n_qtiles - 1)


def _diff_attention(qt, k, vt, lq1, lk1, lq2, lk2, g_col):
    b, s, _ = k.shape
    n_chunks = s // TK
    n_qtiles = s // TQ
    assert n_chunks & (n_chunks - 1) == 0 and n_chunks >= 2
    return pl.pallas_call(
        functools.partial(_attn_kernel, n_chunks=n_chunks, n_qtiles=n_qtiles),
        grid=(b, DIFF_HEADS),
        in_specs=[
            pl.BlockSpec((1, 1, n_qtiles, 2 * DIFF_DQ, TQ), lambda bi, h: (bi, h, 0, 0, 0)),
            pl.BlockSpec((1, s, 2 * DIFF_DQ), lambda bi, h: (bi, 0, h)),
            pl.BlockSpec((1, 1, n_chunks, V_ROWS, TK), lambda bi, h: (bi, h, 0, 0, 0)),
            _const_spec((1, DIFF_DQ)),
            _const_spec((1, DIFF_DQ)),
            _const_spec((1, DIFF_DQ)),
            _const_spec((1, DIFF_DQ)),
            _const_spec((DIFF_DV, 1)),
        ],
        out_specs=pl.BlockSpec((1, s, DIFF_DV), lambda bi, h: (bi, 0, h)),
        out_shape=jax.ShapeDtypeStruct((b, s, DIFF_WIDTH), BF16),
        scratch_shapes=[pltpu.VMEM((TK, 2 * TQ), F32), pltpu.VMEM((TK, 2 * TQ), BF16),
                        pltpu.VMEM((V_ROWS, 2 * TQ), F32)],
        compiler_params=_params(2),
        name="diff_attention",
    )(qt, k, vt, lq1, lk1, lq2, lk2, g_col)


def _memkv_kernel(m_ref, g_ref, w_ref, k_ref, v_ref):
    h = _rms(m_ref[0], g_ref[...]).astype(BF16)
    kv = jnp.dot(h, w_ref[...], preferred_element_type=F32)
    k_ref[0] = kv[:, :D_MODEL].astype(BF16)
    v_ref[0] = kv[:, D_MODEL:].astype(BF16)


def _mem_kv(mem, g, w_mkv):
    b = mem.shape[0]
    blk = pl.BlockSpec((1, N_MEM, D_MODEL), lambda bi: (bi, 0, 0))
    return pl.pallas_call(
        _memkv_kernel,
        grid=(b,),
        in_specs=[blk, _const_spec((1, D_MODEL)), _const_spec((D_MODEL, 2 * D_MODEL))],
        out_specs=[blk, blk],
        out_shape=[jax.ShapeDtypeStruct((b, N_MEM, D_MODEL), BF16)] * 2,
        compiler_params=_params(1),
        name="mem_kv",
    )(mem, g, w_mkv)


def _cross_kernel(x_ref, a_ref, b_ref, wout_ref, gc_ref, wq_ref, km_ref, vm_ref, wo_ref, gf_ref,
                  x2_ref, hf_ref):
    x1 = (x_ref[0]
          + jnp.dot(a_ref[0], wout_ref[:CONV_CH, :], preferred_element_type=F32)
          + jnp.dot(b_ref[0], wout_ref[CONV_CH:, :], preferred_element_type=F32))
    hq = _rms(x1, gc_ref[...]).astype(BF16)
    q = (jnp.dot(hq, wq_ref[...], preferred_element_type=F32) * (MEM_HD ** -0.5)).astype(BF16)
    heads = []
    for h in range(MEM_HEADS):
        sl = slice(h * MEM_HD, (h + 1) * MEM_HD)
        s = lax.dot_general(q[:, sl], km_ref[0, :, sl], (((1,), (1,)), ((), ())),
                            preferred_element_type=F32)
        p = jnp.exp(s - jnp.max(s, axis=-1, keepdims=True))
        l = jnp.sum(p, axis=-1, keepdims=True)
        o = jnp.dot(p.astype(BF16), vm_ref[0, :, sl], preferred_element_type=F32)
        heads.append((o * (1.0 / l)).astype(BF16))
    o_all = jnp.concatenate(heads, axis=-1)
    x2 = x1 + jnp.dot(o_all, wo_ref[...], preferred_element_type=F32)
    x2_ref[0] = x2
    hf_ref[0] = _rms(x2, gf_ref[...]).astype(BF16)


def _cross(x, a, bb, w_out, g_cross, w_mq, k_mem, v_mem, w_mo, g_ffn):
    b, s, _ = x.shape
    tile = lambda width: pl.BlockSpec((1, TM, width), lambda bi, i: (bi, i, 0))
    mem_blk = pl.BlockSpec((1, N_MEM, D_MODEL), lambda bi, i: (bi, 0, 0))
    return pl.pallas_call(
        _cross_kernel,
        grid=(b, s // TM),
        in_specs=[
            tile(D_MODEL), tile(CONV_CH), tile(DIFF_WIDTH),
            _const_spec((D_MODEL, D_MODEL)), _const_spec((1, D_MODEL)),
            _const_spec((D_MODEL, D_MODEL)), mem_blk, mem_blk,
            _const_spec((D_MODEL, D_MODEL)), _const_spec((1, D_MODEL)),
        ],
        out_specs=[tile(D_MODEL), tile(D_MODEL)],
        out_shape=[jax.ShapeDtypeStruct((b, s, D_MODEL), F32),
                   jax.ShapeDtypeStruct((b, s, D_MODEL), BF16)],
        compiler_params=_params(2),
        name="cross_attention",
    )(x, a, bb, w_out, g_cross, w_mq, k_mem, v_mem, w_mo, g_ffn)


def _ffn_kernel(h_ref, hprev_ref, hnext_ref, x2_ref, wup_ref, dw_ref, db_ref, wdown_ref, gfin_ref,
                y_ref, u_ref, gate_ref):
    i = pl.program_id(1)
    n = pl.num_programs(1)
    prev = hprev_ref[0]
    nxt = hnext_ref[0]
    hwin = jnp.concatenate(
        [jnp.where(i > 0, prev, jnp.zeros_like(prev)), h_ref[0],
         jnp.where(i < n - 1, nxt, jnp.zeros_like(nxt))], axis=0)

    def conv3(col0):
        cols = slice(col0, col0 + FF_CHUNK)
        u_ref[...] = jnp.dot(hwin, wup_ref[:, cols], preferred_element_type=F32)
        return (u_ref[HALO - 1:HALO - 1 + TM, :] * dw_ref[0:1, cols]
                + u_ref[HALO:HALO + TM, :] * dw_ref[1:2, cols]
                + u_ref[HALO + 1:HALO + 1 + TM, :] * dw_ref[2:3, cols]
                + db_ref[:, cols])

    for c in range(D_FF // FF_CHUNK):
        val = conv3(c * FF_CHUNK)
        gate = conv3(D_FF + c * FF_CHUNK)
        gate_ref[:, c * FF_CHUNK:(c + 1) * FF_CHUNK] = (gate * jax.nn.sigmoid(gate) * val).astype(BF16)
    x3 = x2_ref[0] + jnp.dot(gate_ref[...], wdown_ref[...], preferred_element_type=F32)
    y_ref[0] = _rms(x3, gfin_ref[...])


def _ffn(hf, x2, w_up, dw_w, dw_b, w_down, g_final):
    b, s, _ = x2.shape
    tile = pl.BlockSpec((1, TM, D_MODEL), lambda bi, i: (bi, i, 0))
    prev_spec, next_spec = _halo_specs(D_MODEL, s)
    return pl.pallas_call(
        _ffn_kernel,
        grid=(b, s // TM),
        in_specs=[
            tile, prev_spec, next_spec, tile,
            _const_spec((D_MODEL, 2 * D_FF)), _const_spec((8, 2 * D_FF)), _const_spec((1, 2 * D_FF)),
            _const_spec((D_FF, D_MODEL)), _const_spec((1, D_MODEL)),
        ],
        out_specs=tile,
        out_shape=jax.ShapeDtypeStruct((b, s, D_MODEL), F32),
        scratch_shapes=[pltpu.VMEM((TM + 2 * HALO, FF_CHUNK), F32), pltpu.VMEM((TM, D_FF), BF16)],
        compiler_params=_params(2),
        name="conv_ffn",
    )(hf, hf, hf, x2, w_up, dw_w, dw_b, w_down, g_final)


def _rope_tables(s):
    inv = ROPE_THETA ** (-jnp.arange(0, DIFF_DQ, 2, dtype=F32) / DIFF_DQ)
    ang = jnp.arange(s, dtype=F32)[:, None] * inv[None, :]
    c, sn = jnp.cos(ang), jnp.sin(ang)
    return jnp.concatenate([c, c, c, c], axis=1), jnp.concatenate([-sn, sn, -sn, sn], axis=1)


def _encoder(x, mem, p):
    cos, sin = _rope_tables(x.shape[1])
    glu, qt, k, vt = _inproj(x, p["norm_mix"], p["w_in"], cos, sin)
    a = _conv_mixer(glu, p["conv_dw_w"], p["conv_dw_b"], p["conv_norm"])
    bb = _diff_attention(qt, k, vt, p["lq1"], p["lk1"], p["lq2"], p["lk2"], p["diff_subln"])
    k_mem, v_mem = _mem_kv(mem, p["norm_mem"], p["w_mkv"])
    x2, hf = _cross(x, a, bb, p["w_out"], p["norm_cross"], p["w_mq"], k_mem, v_mem, p["w_mo"],
                    p["norm_ffn"])
    return _ffn(hf, x2, p["w_up"], p["ffn_dw_w"], p["ffn_dw_b"], p["w_down"], p["norm_final"])


def kernel(x_prompt, x_sample, mem_prompt, mem_sample, norm_mix, w_in, conv_dw_w, conv_dw_b, conv_norm, lambda_q1, lambda_k1, lambda_q2, lambda_k2, diff_subln, w_out, norm_cross, norm_mem, w_mq, w_mkv, w_mo, norm_ffn, w_up, ffn_dw_w, ffn_dw_b, w_down, norm_final):
    row = lambda v: v.reshape(1, -1).astype(F32)
    p = {
        "norm_mix": row(norm_mix[0]),
        "w_in": w_in[0].astype(BF16),
        "conv_dw_w": jnp.pad(conv_dw_w[0], ((0, 1), (0, 0))),
        "conv_dw_b": row(conv_dw_b[0]),
        "conv_norm": row(conv_norm[0]),
        "lq1": row(lambda_q1[0]), "lk1": row(lambda_k1[0]),
        "lq2": row(lambda_q2[0]), "lk2": row(lambda_k2[0]),
        "diff_subln": diff_subln[0].reshape(-1, 1).astype(F32),
        "w_out": w_out[0].astype(BF16),
        "norm_cross": row(norm_cross[0]),
        "norm_mem": row(norm_mem[0]),
        "w_mq": w_mq[0].astype(BF16),
        "w_mkv": w_mkv[0].astype(BF16),
        "w_mo": w_mo[0].astype(BF16),
        "norm_ffn": row(norm_ffn[0]),
        "w_up": w_up[0].astype(BF16),
        "ffn_dw_w": jnp.pad(ffn_dw_w[0], ((0, 5), (0, 0))),
        "ffn_dw_b": row(ffn_dw_b[0]),
        "w_down": w_down[0].astype(BF16),
        "norm_final": row(norm_final),
    }
    return (_encoder(x_prompt, mem_prompt, p), _encoder(x_sample, mem_sample, p))
```

```python
import functools
import math

import jax
import jax.numpy as jnp
from jax import lax
from jax.experimental import pallas as pl
from jax.experimental.pallas import tpu as pltpu

F32 = jnp.float32
BF16 = jnp.bfloat16

D_MODEL = 1024
N_MEM = 256
CONV_CH = 512
CONV_WIDTH = 31
DIFF_HEADS = 4
DIFF_DQ = 64
DIFF_DV = 128
DIFF_WIDTH = DIFF_HEADS * DIFF_DV
IN_COLS = 2 * CONV_CH + 3 * DIFF_WIDTH
MEM_HEADS = 4
MEM_HD = D_MODEL // MEM_HEADS
D_FF = 2816
ROPE_THETA = 10000.0
EPS = 1e-6
LAM_INIT = 0.8 - 0.6 * math.exp(-0.3 * 0)
Q_SCALE = DIFF_DQ ** -0.5 * math.log2(math.e)

TM = 512
TQ = 512
TK = 1024
V_ROWS = DIFF_DV + 16
HALO = 16
CONV_ROWS = 64
FF_CHUNK = 256
VMEM_LIMIT = 56 * 1024 * 1024
NEG_BIG = -1e30


def _rms(x, g):
    ms = jnp.mean(x * x, axis=-1, keepdims=True)
    return x * lax.rsqrt(ms + EPS) * g


def _params(n_axes):
    return pltpu.CompilerParams(dimension_semantics=("arbitrary",) * n_axes,
                                vmem_limit_bytes=VMEM_LIMIT)


def _const_spec(shape):
    return pl.BlockSpec(shape, lambda *_: (0,) * len(shape), pipeline_mode=pl.Buffered(1))


def _inproj_kernel(x_ref, g_ref, w_ref, cos_ref, sin_ref, glu_ref, qt_ref, k_ref, vt_ref):
    h = _rms(x_ref[0], g_ref[...]).astype(BF16)
    z = jnp.dot(h, w_ref[...], preferred_element_type=F32)
    glu_ref[0] = z[:, :CONV_CH] * jax.nn.sigmoid(z[:, CONV_CH:2 * CONV_CH])

    cos = cos_ref[...]
    sin = sin_ref[...]
    lane = lax.broadcasted_iota(jnp.int32, (TM, 128), 1)
    first_half = (lane % DIFF_DQ) < (DIFF_DQ // 2)

    def rope(t):
        partner = jnp.where(first_half, pltpu.roll(t, 96, 1), pltpu.roll(t, 32, 1))
        return t * cos + partner * sin

    q_off = 2 * CONV_CH
    k_off = q_off + DIFF_WIDTH
    v_off = k_off + DIFF_WIDTH
    for j in range(DIFF_HEADS):
        q = rope(z[:, q_off + 128 * j:q_off + 128 * (j + 1)]) * Q_SCALE
        qt_ref[0, j, 0] = q.T.astype(BF16)
        k = rope(z[:, k_off + 128 * j:k_off + 128 * (j + 1)])
        k_ref[0, :, 128 * j:128 * (j + 1)] = k.astype(BF16)
        v = z[:, v_off + 128 * j:v_off + 128 * (j + 1)]
        vt_ref[0, j, 0, 0:DIFF_DV, :] = v.T.astype(BF16)
        pad_row = lax.broadcasted_iota(jnp.int32, (V_ROWS - DIFF_DV, TM), 0)
        vt_ref[0, j, 0, DIFF_DV:V_ROWS, :] = jnp.where(pad_row == 0, 1.0, 0.0).astype(BF16)


def _inproj(x, g, w_in, cos, sin):
    b, s, _ = x.shape
    nt = s // TM
    return pl.pallas_call(
        _inproj_kernel,
        grid=(b, nt),
        in_specs=[
            pl.BlockSpec((1, TM, D_MODEL), lambda bi, i: (bi, i, 0)),
            _const_spec((1, D_MODEL)),
            _const_spec((D_MODEL, IN_COLS)),
            pl.BlockSpec((TM, 128), lambda bi, i: (i, 0)),
            pl.BlockSpec((TM, 128), lambda bi, i: (i, 0)),
        ],
        out_specs=[
            pl.BlockSpec((1, TM, CONV_CH), lambda bi, i: (bi, i, 0)),
            pl.BlockSpec((1, DIFF_HEADS, 1, 2 * DIFF_DQ, TM), lambda bi, i: (bi, 0, i, 0, 0)),
            pl.BlockSpec((1, TM, DIFF_WIDTH), lambda bi, i: (bi, i, 0)),
            pl.BlockSpec((1, DIFF_HEADS, 1, V_ROWS, TM), lambda bi, i: (bi, 0, i, 0, 0)),
        ],
        out_shape=[
            jax.ShapeDtypeStruct((b, s, CONV_CH), F32),
            jax.ShapeDtypeStruct((b, DIFF_HEADS, nt, 2 * DIFF_DQ, TM), BF16),
            jax.ShapeDtypeStruct((b, s, DIFF_WIDTH), BF16),
            jax.ShapeDtypeStruct((b, DIFF_HEADS, nt, V_ROWS, TM), BF16),
        ],
        compiler_params=_params(2),
        name="inproj",
    )(x, g, w_in, cos, sin)


def _conv_kernel(main_ref, prev_ref, next_ref, w_ref, b_ref, g_ref, o_ref, win_ref, sh_ref, y_ref):
    i = pl.program_id(1)
    n = pl.num_programs(1)
    prev = prev_ref[0]
    nxt = next_ref[0]
    win_ref[0:HALO, :] = jnp.where(i > 0, prev, jnp.zeros_like(prev))
    win_ref[HALO:HALO + TM, :] = main_ref[0]
    win_ref[HALO + TM:2 * HALO + TM, :] = jnp.where(i < n - 1, nxt, jnp.zeros_like(nxt))
    rows = TM + 2 * HALO
    lead = HALO - (CONV_WIDTH - 1) // 2
    for cb in range(CONV_CH // 128):
        lanes = slice(cb * 128, (cb + 1) * 128)
        wblk = win_ref[:, lanes]
        for r in range(1, 8):
            sh_ref[r - 1] = pltpu.roll(wblk, rows - r, 0)
        for rb in range(TM // CONV_ROWS):
            base = rb * CONV_ROWS
            acc = jnp.broadcast_to(b_ref[:, lanes], (CONV_ROWS, 128))
            for j in range(CONV_WIDTH):
                q, r = divmod(j + lead, 8)
                start = base + 8 * q
                if r == 0:
                    src = win_ref[start:start + CONV_ROWS, lanes]
                else:
                    src = sh_ref[r - 1, start:start + CONV_ROWS, :]
                acc = acc + src * w_ref[j:j + 1, lanes]
            y_ref[base:base + CONV_ROWS, lanes] = acc
    y = _rms(y_ref[...], g_ref[...])
    o_ref[0] = (y * jax.nn.sigmoid(y)).astype(BF16)


def _halo_specs(width, seq_len):
    per_tile = TM // HALO
    last = seq_len // HALO - 1

    def prev_map(bi, i):
        return (bi, jnp.maximum(i * per_tile - 1, 0), 0)

    def next_map(bi, i):
        return (bi, jnp.minimum((i + 1) * per_tile, last), 0)

    return (pl.BlockSpec((1, HALO, width), prev_map), pl.BlockSpec((1, HALO, width), next_map))


def _conv_mixer(glu, w, bias, g):
    b, s, _ = glu.shape
    prev_spec, next_spec = _halo_specs(CONV_CH, s)
    return pl.pallas_call(
        _conv_kernel,
        grid=(b, s // TM),
        in_specs=[
            pl.BlockSpec((1, TM, CONV_CH), lambda bi, i: (bi, i, 0)),
            prev_spec,
            next_spec,
            _const_spec((CONV_WIDTH + 1, CONV_CH)),
            _const_spec((1, CONV_CH)),
            _const_spec((1, CONV_CH)),
        ],
        out_specs=pl.BlockSpec((1, TM, CONV_CH), lambda bi, i: (bi, i, 0)),
        out_shape=jax.ShapeDtypeStruct((b, s, CONV_CH), BF16),
        scratch_shapes=[pltpu.VMEM((TM + 2 * HALO, CONV_CH), F32),
                        pltpu.VMEM((7, TM + 2 * HALO, 128), F32), pltpu.VMEM((TM, CONV_CH), F32)],
        compiler_params=_params(2),
        name="conv_mixer",
    )(glu, glu, glu, w, bias, g)


def _attn_kernel(qt_ref, k_ref, vt_ref, lq1_ref, lk1_ref, lq2_ref, lk2_ref, g_ref, o_ref,
                 s_ref, p_ref, acc_ref, *, n_chunks, n_qtiles):
    n_items = n_chunks * n_qtiles
    lg = n_chunks.bit_length() - 1
    zeros = jnp.zeros((DIFF_DQ, TQ), BF16)
    lam = (jnp.exp(jnp.sum(lq1_ref[...] * lk1_ref[...], axis=-1, keepdims=True))
           - jnp.exp(jnp.sum(lq2_ref[...] * lk2_ref[...], axis=-1, keepdims=True)) + LAM_INIT)

    def stage1(t, m_prev):
        qi = t >> lg
        c = t & (n_chunks - 1)
        qt = qt_ref[0, 0, qi]
        qq = jnp.concatenate(
            [jnp.concatenate([qt[:DIFF_DQ], zeros], axis=0),
             jnp.concatenate([zeros, qt[DIFF_DQ:]], axis=0)], axis=1)
        off = pl.multiple_of(c * TK, TK)
        s = jnp.dot(k_ref[0, pl.ds(off, TK), :], qq, preferred_element_type=F32)
        s_ref[...] = s
        m_prev = jnp.where(c == 0, NEG_BIG, m_prev)
        m_new = jnp.maximum(m_prev, jnp.max(s, axis=0, keepdims=True))
        return m_new, jnp.exp2(m_prev - m_new)

    def stage2(m):
        p_ref[...] = jnp.exp2(s_ref[...] - m).astype(BF16)

    def stage3(t, a):
        c = t & (n_chunks - 1)
        pv = acc_ref[...] * a
        for i in range(TK // TM):
            pv = pv + jnp.dot(vt_ref[0, 0, (TK // TM) * c + i], p_ref[i * TM:(i + 1) * TM, :],
                              preferred_element_type=F32)
        acc_ref[...] = pv

    def finalize(qi):
        acc = acc_ref[...]
        o = acc[:DIFF_DV, :] * (1.0 / acc[DIFF_DV:DIFF_DV + 1, :])
        d = o[:, :TQ] - lam * o[:, TQ:]
        ms = jnp.mean(d * d, axis=0, keepdims=True)
        y = d * lax.rsqrt(ms + EPS) * g_ref[...] * (1.0 - LAM_INIT)
        o_ref[0, pl.ds(pl.multiple_of(qi * TQ, TQ), TQ), :] = y.T.astype(BF16)

    acc_ref[...] = jnp.zeros_like(acc_ref)
    m0, a0 = stage1(0, jnp.full((1, 2 * TQ), NEG_BIG, F32))
    stage2(m0)
    m1, a1 = stage1(1, m0)

    def step(t, carry, may_finalize):
        m_prev, a_prev1, a_prev2 = carry
        stage3(t - 2, a_prev2)
        stage2(m_prev)
        m_t, a_t = stage1(t, m_prev)
        if may_finalize:
            @pl.when(((t - 2) & (n_chunks - 1)) == n_chunks - 1)
            def _():
                finalize((t - 2) >> lg)

        return m_t, a_t, a_prev1

    def pair(j, carry):
        t = 2 * j + 2
        return step(t + 1, step(t, carry, False), True)

    m_last, a_last1, a_last2 = lax.fori_loop(0, (n_items - 2) // 2, pair, (m1, a1, a0))
    stage3(n_items - 2, a_last2)
    stage2(m_last)
    stage3(n_items - 1, a_last1)
    finalize(n_qtiles - 1)


def _diff_attention(qt, k, vt, lq1, lk1, lq2, lk2, g_col):
    b, s, _ = k.shape
    n_chunks = s // TK
    n_qtiles = s // TQ
    assert n_chunks & (n_chunks - 1) == 0 and n_chunks >= 2
    return pl.pallas_call(
        functools.partial(_attn_kernel, n_chunks=n_chunks, n_qtiles=n_qtiles),
        grid=(b, DIFF_HEADS),
        in_specs=[
            pl.BlockSpec((1, 1, n_qtiles, 2 * DIFF_DQ, TQ), lambda bi, h: (bi, h, 0, 0, 0)),
            pl.BlockSpec((1, s, 2 * DIFF_DQ), lambda bi, h: (bi, 0, h)),
            pl.BlockSpec((1, 1, s // TM, V_ROWS, TM), lambda bi, h: (bi, h, 0, 0, 0)),
            _const_spec((1, DIFF_DQ)),
            _const_spec((1, DIFF_DQ)),
            _const_spec((1, DIFF_DQ)),
            _const_spec((1, DIFF_DQ)),
            _const_spec((DIFF_DV, 1)),
        ],
        out_specs=pl.BlockSpec((1, s, DIFF_DV), lambda bi, h: (bi, 0, h)),
        out_shape=jax.ShapeDtypeStruct((b, s, DIFF_WIDTH), BF16),
        scratch_shapes=[pltpu.VMEM((TK, 2 * TQ), F32), pltpu.VMEM((TK, 2 * TQ), BF16),
                        pltpu.VMEM((V_ROWS, 2 * TQ), F32)],
        compiler_params=_params(2),
        name="diff_attention",
    )(qt, k, vt, lq1, lk1, lq2, lk2, g_col)


def _memkv_kernel(m_ref, g_ref, w_ref, k_ref, v_ref):
    h = _rms(m_ref[0], g_ref[...]).astype(BF16)
    kv = jnp.dot(h, w_ref[...], preferred_element_type=F32)
    k_ref[0] = kv[:, :D_MODEL].astype(BF16)
    v_ref[0] = kv[:, D_MODEL:].astype(BF16)


def _mem_kv(mem, g, w_mkv):
    b = mem.shape[0]
    blk = pl.BlockSpec((1, N_MEM, D_MODEL), lambda bi: (bi, 0, 0))
    return pl.pallas_call(
        _memkv_kernel,
        grid=(b,),
        in_specs=[blk, _const_spec((1, D_MODEL)), _const_spec((D_MODEL, 2 * D_MODEL))],
        out_specs=[blk, blk],
        out_shape=[jax.ShapeDtypeStruct((b, N_MEM, D_MODEL), BF16)] * 2,
        compiler_params=_params(1),
        name="mem_kv",
    )(mem, g, w_mkv)


def _cross_kernel(x_ref, a_ref, b_ref, wout_ref, gc_ref, wq_ref, km_ref, vm_ref, wo_ref, gf_ref,
                  x2_ref, hf_ref):
    x1 = (x_ref[0]
          + jnp.dot(a_ref[0], wout_ref[:CONV_CH, :], preferred_element_type=F32)
          + jnp.dot(b_ref[0], wout_ref[CONV_CH:, :], preferred_element_type=F32))
    hq = _rms(x1, gc_ref[...]).astype(BF16)
    q = (jnp.dot(hq, wq_ref[...], preferred_element_type=F32) * (MEM_HD ** -0.5)).astype(BF16)
    heads = []
    for h in range(MEM_HEADS):
        sl = slice(h * MEM_HD, (h + 1) * MEM_HD)
        s = lax.dot_general(q[:, sl], km_ref[0, :, sl], (((1,), (1,)), ((), ())),
                            preferred_element_type=F32)
        p = jnp.exp(s - jnp.max(s, axis=-1, keepdims=True))
        l = jnp.sum(p, axis=-1, keepdims=True)
        o = jnp.dot(p.astype(BF16), vm_ref[0, :, sl], preferred_element_type=F32)
        heads.append((o * (1.0 / l)).astype(BF16))
    o_all = jnp.concatenate(heads, axis=-1)
    x2 = x1 + jnp.dot(o_all, wo_ref[...], preferred_element_type=F32)
    x2_ref[0] = x2
    hf_ref[0] = _rms(x2, gf_ref[...]).astype(BF16)


def _cross(x, a, bb, w_out, g_cross, w_mq, k_mem, v_mem, w_mo, g_ffn):
    b, s, _ = x.shape
    tile = lambda width: pl.BlockSpec((1, TM, width), lambda bi, i: (bi, i, 0))
    mem_blk = pl.BlockSpec((1, N_MEM, D_MODEL), lambda bi, i: (bi, 0, 0))
    return pl.pallas_call(
        _cross_kernel,
        grid=(b, s // TM),
        in_specs=[
            tile(D_MODEL), tile(CONV_CH), tile(DIFF_WIDTH),
            _const_spec((D_MODEL, D_MODEL)), _const_spec((1, D_MODEL)),
            _const_spec((D_MODEL, D_MODEL)), mem_blk, mem_blk,
            _const_spec((D_MODEL, D_MODEL)), _const_spec((1, D_MODEL)),
        ],
        out_specs=[tile(D_MODEL), tile(D_MODEL)],
        out_shape=[jax.ShapeDtypeStruct((b, s, D_MODEL), F32),
                   jax.ShapeDtypeStruct((b, s, D_MODEL), BF16)],
        compiler_params=_params(2),
        name="cross_attention",
    )(x, a, bb, w_out, g_cross, w_mq, k_mem, v_mem, w_mo, g_ffn)


def _ffn_kernel(h_ref, hprev_ref, hnext_ref, x2_ref, wup_ref, dw_ref, db_ref, wdown_ref, gfin_ref,
                y_ref, u_ref, gate_ref):
    i = pl.program_id(1)
    n = pl.num_programs(1)
    prev = hprev_ref[0]
    nxt = hnext_ref[0]
    hwin = jnp.concatenate(
        [jnp.where(i > 0, prev, jnp.zeros_like(prev)), h_ref[0],
         jnp.where(i < n - 1, nxt, jnp.zeros_like(nxt))], axis=0)

    def conv3(col0):
        cols = slice(col0, col0 + FF_CHUNK)
        u_ref[...] = jnp.dot(hwin, wup_ref[:, cols], preferred_element_type=F32)
        return (u_ref[HALO - 1:HALO - 1 + TM, :] * dw_ref[0:1, cols]
                + u_ref[HALO:HALO + TM, :] * dw_ref[1:2, cols]
                + u_ref[HALO + 1:HALO + 1 + TM, :] * dw_ref[2:3, cols]
                + db_ref[:, cols])

    for c in range(D_FF // FF_CHUNK):
        val = conv3(c * FF_CHUNK)
        gate = conv3(D_FF + c * FF_CHUNK)
        gate_ref[:, c * FF_CHUNK:(c + 1) * FF_CHUNK] = (gate * jax.nn.sigmoid(gate) * val).astype(BF16)
    x3 = x2_ref[0] + jnp.dot(gate_ref[...], wdown_ref[...], preferred_element_type=F32)
    y_ref[0] = _rms(x3, gfin_ref[...])


def _ffn(hf, x2, w_up, dw_w, dw_b, w_down, g_final):
    b, s, _ = x2.shape
    tile = pl.BlockSpec((1, TM, D_MODEL), lambda bi, i: (bi, i, 0))
    prev_spec, next_spec = _halo_specs(D_MODEL, s)
    return pl.pallas_call(
        _ffn_kernel,
        grid=(b, s // TM),
        in_specs=[
            tile, prev_spec, next_spec, tile,
            _const_spec((D_MODEL, 2 * D_FF)), _const_spec((8, 2 * D_FF)), _const_spec((1, 2 * D_FF)),
            _const_spec((D_FF, D_MODEL)), _const_spec((1, D_MODEL)),
        ],
        out_specs=tile,
        out_shape=jax.ShapeDtypeStruct((b, s, D_MODEL), F32),
        scratch_shapes=[pltpu.VMEM((TM + 2 * HALO, FF_CHUNK), F32), pltpu.VMEM((TM, D_FF), BF16)],
        compiler_params=_params(2),
        name="conv_ffn",
    )(hf, hf, hf, x2, w_up, dw_w, dw_b, w_down, g_final)


def _rope_tables(s):
    inv = ROPE_THETA ** (-jnp.arange(0, DIFF_DQ, 2, dtype=F32) / DIFF_DQ)
    ang = jnp.arange(s, dtype=F32)[:, None] * inv[None, :]
    c, sn = jnp.cos(ang), jnp.sin(ang)
    return jnp.concatenate([c, c, c, c], axis=1), jnp.concatenate([-sn, sn, -sn, sn], axis=1)


def _encoder(x, mem, p):
    cos, sin = _rope_tables(x.shape[1])
    glu, qt, k, vt = _inproj(x, p["norm_mix"], p["w_in"], cos, sin)
    a = _conv_mixer(glu, p["conv_dw_w"], p["conv_dw_b"], p["conv_norm"])
    bb = _diff_attention(qt, k, vt, p["lq1"], p["lk1"], p["lq2"], p["lk2"], p["diff_subln"])
    k_mem, v_mem = _mem_kv(mem, p["norm_mem"], p["w_mkv"])
    x2, hf = _cross(x, a, bb, p["w_out"], p["norm_cross"], p["w_mq"], k_mem, v_mem, p["w_mo"],
                    p["norm_ffn"])
    return _ffn(hf, x2, p["w_up"], p["ffn_dw_w"], p["ffn_dw_b"], p["w_down"], p["norm_final"])


def kernel(x_prompt, x_sample, mem_prompt, mem_sample, norm_mix, w_in, conv_dw_w, conv_dw_b, conv_norm, lambda_q1, lambda_k1, lambda_q2, lambda_k2, diff_subln, w_out, norm_cross, norm_mem, w_mq, w_mkv, w_mo, norm_ffn, w_up, ffn_dw_w, ffn_dw_b, w_down, norm_final):
    row = lambda v: v.reshape(1, -1).astype(F32)
    p = {
        "norm_mix": row(norm_mix[0]),
        "w_in": w_in[0].astype(BF16),
        "conv_dw_w": jnp.pad(conv_dw_w[0], ((0, 1), (0, 0))),
        "conv_dw_b": row(conv_dw_b[0]),
        "conv_norm": row(conv_norm[0]),
        "lq1": row(lambda_q1[0]), "lk1": row(lambda_k1[0]),
        "lq2": row(lambda_q2[0]), "lk2": row(lambda_k2[0]),
        "diff_subln": diff_subln[0].reshape(-1, 1).astype(F32),
        "w_out": w_out[0].astype(BF16),
        "norm_cross": row(norm_cross[0]),
        "norm_mem": row(norm_mem[0]),
        "w_mq": w_mq[0].astype(BF16),
        "w_mkv": w_mkv[0].astype(BF16),
        "w_mo": w_mo[0].astype(BF16),
        "norm_ffn": row(norm_ffn[0]),
        "w_up": w_up[0].astype(BF16),
        "ffn_dw_w": jnp.pad(ffn_dw_w[0], ((0, 5), (0, 0))),
        "ffn_dw_b": row(ffn_dw_b[0]),
        "w_down": w_down[0].astype(BF16),
        "norm_final": row(norm_final),
    }
    return (_encoder(x_prompt, mem_prompt, p), _encoder(x_sample, mem_sample, p))
```

```python
import functools
import math

import jax
import jax.numpy as jnp
from jax import lax
from jax.experimental import pallas as pl
from jax.experimental.pallas import tpu as pltpu

F32 = jnp.float32
BF16 = jnp.bfloat16

D_MODEL = 1024
N_MEM = 256
CONV_CH = 512
CONV_WIDTH = 31
DIFF_HEADS = 4
DIFF_DQ = 64
DIFF_DV = 128
DIFF_WIDTH = DIFF_HEADS * DIFF_DV
IN_COLS = 2 * CONV_CH + 3 * DIFF_WIDTH
MEM_HEADS = 4
MEM_HD = D_MODEL // MEM_HEADS
D_FF = 2816
ROPE_THETA = 10000.0
EPS = 1e-6
LAM_INIT = 0.8 - 0.6 * math.exp(-0.3 * 0)
Q_SCALE = DIFF_DQ ** -0.5 * math.log2(math.e)

TM = 512
TQ = 512
ATTN_CHUNKS = 4
V_ROWS = DIFF_DV + 16
HALO = 16
CONV_ROWS = 64
FF_CHUNK = 256
VMEM_LIMIT = 56 * 1024 * 1024
NEG_BIG = -1e30


def _rms(x, g):
    ms = jnp.mean(x * x, axis=-1, keepdims=True)
    return x * lax.rsqrt(ms + EPS) * g


def _params(n_axes):
    return pltpu.CompilerParams(dimension_semantics=("arbitrary",) * n_axes,
                                vmem_limit_bytes=VMEM_LIMIT)


def _const_spec(shape):
    return pl.BlockSpec(shape, lambda *_: (0,) * len(shape), pipeline_mode=pl.Buffered(1))


def _inproj_kernel(x_ref, g_ref, w_ref, cos_ref, sin_ref, glu_ref, qt_ref, k_ref, vt_ref):
    h = _rms(x_ref[0], g_ref[...]).astype(BF16)
    z = jnp.dot(h, w_ref[...], preferred_element_type=F32)
    glu_ref[0] = z[:, :CONV_CH] * jax.nn.sigmoid(z[:, CONV_CH:2 * CONV_CH])

    cos = cos_ref[...]
    sin = sin_ref[...]
    lane = lax.broadcasted_iota(jnp.int32, (TM, 128), 1)
    first_half = (lane % DIFF_DQ) < (DIFF_DQ // 2)

    def rope(t):
        partner = jnp.where(first_half, pltpu.roll(t, 96, 1), pltpu.roll(t, 32, 1))
        return t * cos + partner * sin

    q_off = 2 * CONV_CH
    k_off = q_off + DIFF_WIDTH
    v_off = k_off + DIFF_WIDTH
    for j in range(DIFF_HEADS):
        q = rope(z[:, q_off + 128 * j:q_off + 128 * (j + 1)]) * Q_SCALE
        qt_ref[0, j, 0] = q.T.astype(BF16)
        k = rope(z[:, k_off + 128 * j:k_off + 128 * (j + 1)])
        k_ref[0, :, 128 * j:128 * (j + 1)] = k.astype(BF16)
        v = z[:, v_off + 128 * j:v_off + 128 * (j + 1)]
        vt_ref[0, j, 0, 0:DIFF_DV, :] = v.T.astype(BF16)
        pad_row = lax.broadcasted_iota(jnp.int32, (V_ROWS - DIFF_DV, TM), 0)
        vt_ref[0, j, 0, DIFF_DV:V_ROWS, :] = jnp.where(pad_row == 0, 1.0, 0.0).astype(BF16)


def _inproj(x, g, w_in, cos, sin):
    b, s, _ = x.shape
    nt = s // TM
    return pl.pallas_call(
        _inproj_kernel,
        grid=(b, nt),
        in_specs=[
            pl.BlockSpec((1, TM, D_MODEL), lambda bi, i: (bi, i, 0)),
            _const_spec((1, D_MODEL)),
            _const_spec((D_MODEL, IN_COLS)),
            pl.BlockSpec((TM, 128), lambda bi, i: (i, 0)),
            pl.BlockSpec((TM, 128), lambda bi, i: (i, 0)),
        ],
        out_specs=[
            pl.BlockSpec((1, TM, CONV_CH), lambda bi, i: (bi, i, 0)),
            pl.BlockSpec((1, DIFF_HEADS, 1, 2 * DIFF_DQ, TM), lambda bi, i: (bi, 0, i, 0, 0)),
            pl.BlockSpec((1, TM, DIFF_WIDTH), lambda bi, i: (bi, i, 0)),
            pl.BlockSpec((1, DIFF_HEADS, 1, V_ROWS, TM), lambda bi, i: (bi, 0, i, 0, 0)),
        ],
        out_shape=[
            jax.ShapeDtypeStruct((b, s, CONV_CH), F32),
            jax.ShapeDtypeStruct((b, DIFF_HEADS, nt, 2 * DIFF_DQ, TM), BF16),
            jax.ShapeDtypeStruct((b, s, DIFF_WIDTH), BF16),
            jax.ShapeDtypeStruct((b, DIFF_HEADS, nt, V_ROWS, TM), BF16),
        ],
        compiler_params=_params(2),
        name="inproj",
    )(x, g, w_in, cos, sin)


def _conv_kernel(main_ref, prev_ref, next_ref, w_ref, b_ref, g_ref, o_ref, win_ref, sh_ref, y_ref):
    i = pl.program_id(1)
    n = pl.num_programs(1)
    prev = prev_ref[0]
    nxt = next_ref[0]
    win_ref[0:HALO, :] = jnp.where(i > 0, prev, jnp.zeros_like(prev))
    win_ref[HALO:HALO + TM, :] = main_ref[0]
    win_ref[HALO + TM:2 * HALO + TM, :] = jnp.where(i < n - 1, nxt, jnp.zeros_like(nxt))
    rows = TM + 2 * HALO
    lead = HALO - (CONV_WIDTH - 1) // 2
    for cb in range(CONV_CH // 128):
        lanes = slice(cb * 128, (cb + 1) * 128)
        wblk = win_ref[:, lanes]
        for r in range(1, 8):
            sh_ref[r - 1] = pltpu.roll(wblk, rows - r, 0)
        for rb in range(TM // CONV_ROWS):
            base = rb * CONV_ROWS
            acc = jnp.broadcast_to(b_ref[:, lanes], (CONV_ROWS, 128))
            for j in range(CONV_WIDTH):
                q, r = divmod(j + lead, 8)
                start = base + 8 * q
                if r == 0:
                    src = win_ref[start:start + CONV_ROWS, lanes]
                else:
                    src = sh_ref[r - 1, start:start + CONV_ROWS, :]
                acc = acc + src * w_ref[j:j + 1, lanes]
            y_ref[base:base + CONV_ROWS, lanes] = acc
    y = _rms(y_ref[...], g_ref[...])
    o_ref[0] = (y * jax.nn.sigmoid(y)).astype(BF16)


def _halo_specs(width, seq_len):
    per_tile = TM // HALO
    last = seq_len // HALO - 1

    def prev_map(bi, i):
        return (bi, jnp.maximum(i * per_tile - 1, 0), 0)

    def next_map(bi, i):
        return (bi, jnp.minimum((i + 1) * per_tile, last), 0)

    return (pl.BlockSpec((1, HALO, width), prev_map), pl.BlockSpec((1, HALO, width), next_map))


def _conv_mixer(glu, w, bias, g):
    b, s, _ = glu.shape
    prev_spec, next_spec = _halo_specs(CONV_CH, s)
    return pl.pallas_call(
        _conv_kernel,
        grid=(b, s // TM),
        in_specs=[
            pl.BlockSpec((1, TM, CONV_CH), lambda bi, i: (bi, i, 0)),
            prev_spec,
            next_spec,
            _const_spec((CONV_WIDTH + 1, CONV_CH)),
            _const_spec((1, CONV_CH)),
            _const_spec((1, CONV_CH)),
        ],
        out_specs=pl.BlockSpec((1, TM, CONV_CH), lambda bi, i: (bi, i, 0)),
        out_shape=jax.ShapeDtypeStruct((b, s, CONV_CH), BF16),
        scratch_shapes=[pltpu.VMEM((TM + 2 * HALO, CONV_CH), F32),
                        pltpu.VMEM((7, TM + 2 * HALO, 128), F32), pltpu.VMEM((TM, CONV_CH), F32)],
        compiler_params=_params(2),
        name="conv_mixer",
    )(glu, glu, glu, w, bias, g)


def _attn_kernel(qt_ref, k_ref, vt_ref, lq1_ref, lk1_ref, lq2_ref, lk2_ref, g_ref, o_ref,
                 s_ref, p_ref, acc_ref, *, n_chunks, n_qtiles):
    n_items = n_chunks * n_qtiles
    lg = n_chunks.bit_length() - 1
    tk = s_ref.shape[0]
    zeros = jnp.zeros((DIFF_DQ, TQ), BF16)
    lam = (jnp.exp(jnp.sum(lq1_ref[...] * lk1_ref[...], axis=-1, keepdims=True))
           - jnp.exp(jnp.sum(lq2_ref[...] * lk2_ref[...], axis=-1, keepdims=True)) + LAM_INIT)

    def stage1(t, m_prev):
        qi = t >> lg
        c = t & (n_chunks - 1)
        qt = qt_ref[0, 0, qi]
        qq = jnp.concatenate(
            [jnp.concatenate([qt[:DIFF_DQ], zeros], axis=0),
             jnp.concatenate([zeros, qt[DIFF_DQ:]], axis=0)], axis=1)
        off = pl.multiple_of(c * tk, tk)
        s = jnp.dot(k_ref[0, pl.ds(off, tk), :], qq, preferred_element_type=F32)
        s_ref[...] = s
        m_prev = jnp.where(c == 0, NEG_BIG, m_prev)
        m_new = jnp.maximum(m_prev, jnp.max(s, axis=0, keepdims=True))
        return m_new, jnp.exp2(m_prev - m_new)

    def stage2(m):
        p_ref[...] = jnp.exp2(s_ref[...] - m).astype(BF16)

    def stage3(t, a):
        c = t & (n_chunks - 1)
        pv = acc_ref[...] * a
        for i in range(tk // TM):
            pv = pv + jnp.dot(vt_ref[0, 0, (tk // TM) * c + i], p_ref[i * TM:(i + 1) * TM, :],
                              preferred_element_type=F32)
        acc_ref[...] = pv

    def finalize(qi):
        acc = acc_ref[...]
        o = acc[:DIFF_DV, :] * (1.0 / acc[DIFF_DV:DIFF_DV + 1, :])
        d = o[:, :TQ] - lam * o[:, TQ:]
        ms = jnp.mean(d * d, axis=0, keepdims=True)
        y = d * lax.rsqrt(ms + EPS) * g_ref[...] * (1.0 - LAM_INIT)
        o_ref[0, pl.ds(pl.multiple_of(qi * TQ, TQ), TQ), :] = y.T.astype(BF16)

    acc_ref[...] = jnp.zeros_like(acc_ref)
    m0, a0 = stage1(0, jnp.full((1, 2 * TQ), NEG_BIG, F32))
    stage2(m0)
    m1, a1 = stage1(1, m0)

    def step(t, carry, may_finalize):
        m_prev, a_prev1, a_prev2 = carry
        stage3(t - 2, a_prev2)
        stage2(m_prev)
        m_t, a_t = stage1(t, m_prev)
        if may_finalize:
            @pl.when(((t - 2) & (n_chunks - 1)) == n_chunks - 1)
            def _():
                finalize((t - 2) >> lg)

        return m_t, a_t, a_prev1

    def pair(j, carry):
        t = 2 * j + 2
        return step(t + 1, step(t, carry, False), True)

    m_last, a_last1, a_last2 = lax.fori_loop(0, (n_items - 2) // 2, pair, (m1, a1, a0))
    stage3(n_items - 2, a_last2)
    stage2(m_last)
    stage3(n_items - 1, a_last1)
    finalize(n_qtiles - 1)


def _diff_attention(qt, k, vt, lq1, lk1, lq2, lk2, g_col):
    b, s, _ = k.shape
    n_chunks = ATTN_CHUNKS
    tk = s // n_chunks
    n_qtiles = s // TQ
    assert n_chunks & (n_chunks - 1) == 0 and n_chunks >= 2 and tk % TM == 0
    return pl.pallas_call(
        functools.partial(_attn_kernel, n_chunks=n_chunks, n_qtiles=n_qtiles),
        grid=(b, DIFF_HEADS),
        in_specs=[
            pl.BlockSpec((1, 1, n_qtiles, 2 * DIFF_DQ, TQ), lambda bi, h: (bi, h, 0, 0, 0)),
            pl.BlockSpec((1, s, 2 * DIFF_DQ), lambda bi, h: (bi, 0, h)),
            pl.BlockSpec((1, 1, s // TM, V_ROWS, TM), lambda bi, h: (bi, h, 0, 0, 0)),
            _const_spec((1, DIFF_DQ)),
            _const_spec((1, DIFF_DQ)),
            _const_spec((1, DIFF_DQ)),
            _const_spec((1, DIFF_DQ)),
            _const_spec((DIFF_DV, 1)),
        ],
        out_specs=pl.BlockSpec((1, s, DIFF_DV), lambda bi, h: (bi, 0, h)),
        out_shape=jax.ShapeDtypeStruct((b, s, DIFF_WIDTH), BF16),
        scratch_shapes=[pltpu.VMEM((tk, 2 * TQ), F32), pltpu.VMEM((tk, 2 * TQ), BF16),
                        pltpu.VMEM((V_ROWS, 2 * TQ), F32)],
        compiler_params=_params(2),
        name="diff_attention",
    )(qt, k, vt, lq1, lk1, lq2, lk2, g_col)


def _memkv_kernel(m_ref, g_ref, w_ref, k_ref, v_ref):
    h = _rms(m_ref[0], g_ref[...]).astype(BF16)
    kv = jnp.dot(h, w_ref[...], preferred_element_type=F32)
    k_ref[0] = kv[:, :D_MODEL].astype(BF16)
    v_ref[0] = kv[:, D_MODEL:].astype(BF16)


def _mem_kv(mem, g, w_mkv):
    b = mem.shape[0]
    blk = pl.BlockSpec((1, N_MEM, D_MODEL), lambda bi: (bi, 0, 0))
    return pl.pallas_call(
        _memkv_kernel,
        grid=(b,),
        in_specs=[blk, _const_spec((1, D_MODEL)), _const_spec((D_MODEL, 2 * D_MODEL))],
        out_specs=[blk, blk],
        out_shape=[jax.ShapeDtypeStruct((b, N_MEM, D_MODEL), BF16)] * 2,
        compiler_params=_params(1),
        name="mem_kv",
    )(mem, g, w_mkv)


def _cross_kernel(x_ref, a_ref, b_ref, wout_ref, gc_ref, wq_ref, km_ref, vm_ref, wo_ref, gf_ref,
                  x2_ref, hf_ref):
    x1 = (x_ref[0]
          + jnp.dot(a_ref[0], wout_ref[:CONV_CH, :], preferred_element_type=F32)
          + jnp.dot(b_ref[0], wout_ref[CONV_CH:, :], preferred_element_type=F32))
    hq = _rms(x1, gc_ref[...]).astype(BF16)
    q = (jnp.dot(hq, wq_ref[...], preferred_element_type=F32) * (MEM_HD ** -0.5)).astype(BF16)
    heads = []
    for h in range(MEM_HEADS):
        sl = slice(h * MEM_HD, (h + 1) * MEM_HD)
        s = lax.dot_general(q[:, sl], km_ref[0, :, sl], (((1,), (1,)), ((), ())),
                            preferred_element_type=F32)
        p = jnp.exp(s - jnp.max(s, axis=-1, keepdims=True))
        l = jnp.sum(p, axis=-1, keepdims=True)
        o = jnp.dot(p.astype(BF16), vm_ref[0, :, sl], preferred_element_type=F32)
        heads.append((o * (1.0 / l)).astype(BF16))
    o_all = jnp.concatenate(heads, axis=-1)
    x2 = x1 + jnp.dot(o_all, wo_ref[...], preferred_element_type=F32)
    x2_ref[0] = x2
    hf_ref[0] = _rms(x2, gf_ref[...]).astype(BF16)


def _cross(x, a, bb, w_out, g_cross, w_mq, k_mem, v_mem, w_mo, g_ffn):
    b, s, _ = x.shape
    tile = lambda width: pl.BlockSpec((1, TM, width), lambda bi, i: (bi, i, 0))
    mem_blk = pl.BlockSpec((1, N_MEM, D_MODEL), lambda bi, i: (bi, 0, 0))
    return pl.pallas_call(
        _cross_kernel,
        grid=(b, s // TM),
        in_specs=[
            tile(D_MODEL), tile(CONV_CH), tile(DIFF_WIDTH),
            _const_spec((D_MODEL, D_MODEL)), _const_spec((1, D_MODEL)),
            _const_spec((D_MODEL, D_MODEL)), mem_blk, mem_blk,
            _const_spec((D_MODEL, D_MODEL)), _const_spec((1, D_MODEL)),
        ],
        out_specs=[tile(D_MODEL), tile(D_MODEL)],
        out_shape=[jax.ShapeDtypeStruct((b, s, D_MODEL), F32),
                   jax.ShapeDtypeStruct((b, s, D_MODEL), BF16)],
        compiler_params=_params(2),
        name="cross_attention",
    )(x, a, bb, w_out, g_cross, w_mq, k_mem, v_mem, w_mo, g_ffn)


def _ffn_kernel(h_ref, hprev_ref, hnext_ref, x2_ref, wup_ref, dw_ref, db_ref, wdown_ref, gfin_ref,
                y_ref, gate_ref):
    i = pl.program_id(1)
    n = pl.num_programs(1)
    prev = hprev_ref[0]
    nxt = hnext_ref[0]
    hwin = jnp.concatenate(
        [jnp.where(i > 0, prev, jnp.zeros_like(prev)), h_ref[0],
         jnp.where(i < n - 1, nxt, jnp.zeros_like(nxt))], axis=0)

    def conv3(col0):
        cols = slice(col0, col0 + FF_CHUNK)
        u = jnp.dot(hwin, wup_ref[:, cols], preferred_element_type=F32)
        return (pltpu.roll(u, 1, 0)[HALO:HALO + TM] * dw_ref[0:1, cols]
                + u[HALO:HALO + TM] * dw_ref[1:2, cols]
                + pltpu.roll(u, TM + 2 * HALO - 1, 0)[HALO:HALO + TM] * dw_ref[2:3, cols]
                + db_ref[:, cols])

    for c in range(D_FF // FF_CHUNK):
        val = conv3(c * FF_CHUNK)
        gate = conv3(D_FF + c * FF_CHUNK)
        gate_ref[:, c * FF_CHUNK:(c + 1) * FF_CHUNK] = (gate * jax.nn.sigmoid(gate) * val).astype(BF16)
    x3 = x2_ref[0] + jnp.dot(gate_ref[...], wdown_ref[...], preferred_element_type=F32)
    y_ref[0] = _rms(x3, gfin_ref[...])


def _ffn(hf, x2, w_up, dw_w, dw_b, w_down, g_final):
    b, s, _ = x2.shape
    tile = pl.BlockSpec((1, TM, D_MODEL), lambda bi, i: (bi, i, 0))
    prev_spec, next_spec = _halo_specs(D_MODEL, s)
    return pl.pallas_call(
        _ffn_kernel,
        grid=(b, s // TM),
        in_specs=[
            tile, prev_spec, next_spec, tile,
            _const_spec((D_MODEL, 2 * D_FF)), _const_spec((8, 2 * D_FF)), _const_spec((1, 2 * D_FF)),
            _const_spec((D_FF, D_MODEL)), _const_spec((1, D_MODEL)),
        ],
        out_specs=tile,
        out_shape=jax.ShapeDtypeStruct((b, s, D_MODEL), F32),
        scratch_shapes=[pltpu.VMEM((TM, D_FF), BF16)],
        compiler_params=_params(2),
        name="conv_ffn",
    )(hf, hf, hf, x2, w_up, dw_w, dw_b, w_down, g_final)


def _rope_tables(s):
    inv = ROPE_THETA ** (-jnp.arange(0, DIFF_DQ, 2, dtype=F32) / DIFF_DQ)
    ang = jnp.arange(s, dtype=F32)[:, None] * inv[None, :]
    c, sn = jnp.cos(ang), jnp.sin(ang)
    return jnp.concatenate([c, c, c, c], axis=1), jnp.concatenate([-sn, sn, -sn, sn], axis=1)


def _encoder(x, mem, p):
    cos, sin = _rope_tables(x.shape[1])
    glu, qt, k, vt = _inproj(x, p["norm_mix"], p["w_in"], cos, sin)
    a = _conv_mixer(glu, p["conv_dw_w"], p["conv_dw_b"], p["conv_norm"])
    bb = _diff_attention(qt, k, vt, p["lq1"], p["lk1"], p["lq2"], p["lk2"], p["diff_subln"])
    k_mem, v_mem = _mem_kv(mem, p["norm_mem"], p["w_mkv"])
    x2, hf = _cross(x, a, bb, p["w_out"], p["norm_cross"], p["w_mq"], k_mem, v_mem, p["w_mo"],
                    p["norm_ffn"])
    return _ffn(hf, x2, p["w_up"], p["ffn_dw_w"], p["ffn_dw_b"], p["w_down"], p["norm_final"])


def kernel(x_prompt, x_sample, mem_prompt, mem_sample, norm_mix, w_in, conv_dw_w, conv_dw_b, conv_norm, lambda_q1, lambda_k1, lambda_q2, lambda_k2, diff_subln, w_out, norm_cross, norm_mem, w_mq, w_mkv, w_mo, norm_ffn, w_up, ffn_dw_w, ffn_dw_b, w_down, norm_final):
    row = lambda v: v.reshape(1, -1).astype(F32)
    p = {
        "norm_mix": row(norm_mix[0]),
        "w_in": w_in[0].astype(BF16),
        "conv_dw_w": jnp.pad(conv_dw_w[0], ((0, 1), (0, 0))),
        "conv_dw_b": row(conv_dw_b[0]),
        "conv_norm": row(conv_norm[0]),
        "lq1": row(lambda_q1[0]), "lk1": row(lambda_k1[0]),
        "lq2": row(lambda_q2[0]), "lk2": row(lambda_k2[0]),
        "diff_subln": diff_subln[0].reshape(-1, 1).astype(F32),
        "w_out": w_out[0].astype(BF16),
        "norm_cross": row(norm_cross[0]),
        "norm_mem": row(norm_mem[0]),
        "w_mq": w_mq[0].astype(BF16),
        "w_mkv": w_mkv[0].astype(BF16),
        "w_mo": w_mo[0].astype(BF16),
        "norm_ffn": row(norm_ffn[0]),
        "w_up": w_up[0].astype(BF16),
        "ffn_dw_w": jnp.pad(ffn_dw_w[0], ((0, 5), (0, 0))),
        "ffn_dw_b": row(ffn_dw_b[0]),
        "w_down": w_down[0].astype(BF16),
        "norm_final": row(norm_final),
    }
    return (_encoder(x_prompt, mem_prompt, p), _encoder(x_sample, mem_sample, p))
```

```python
import functools
import math

import jax
import jax.numpy as jnp
from jax import lax
from jax.experimental import pallas as pl
from jax.experimental.pallas import tpu as pltpu

F32 = jnp.float32
BF16 = jnp.bfloat16

D_MODEL = 1024
N_MEM = 256
CONV_CH = 512
CONV_WIDTH = 31
DIFF_HEADS = 4
DIFF_DQ = 64
DIFF_DV = 128
DIFF_WIDTH = DIFF_HEADS * DIFF_DV
IN_COLS = 2 * CONV_CH + 3 * DIFF_WIDTH
MEM_HEADS = 4
MEM_HD = D_MODEL // MEM_HEADS
D_FF = 2816
ROPE_THETA = 10000.0
EPS = 1e-6
LAM_INIT = 0.8 - 0.6 * math.exp(-0.3 * 0)
Q_SCALE = DIFF_DQ ** -0.5 * math.log2(math.e)

TM = 512
TQ = 512
TK = 2048
V_ROWS = DIFF_DV + 16
HALO = 16
CONV_ROWS = 64
FF_CHUNK = 256
VMEM_LIMIT = 56 * 1024 * 1024
NEG_BIG = -1e30


def _rms(x, g):
    ms = jnp.mean(x * x, axis=-1, keepdims=True)
    return x * lax.rsqrt(ms + EPS) * g


def _params(n_axes):
    return pltpu.CompilerParams(dimension_semantics=("arbitrary",) * n_axes,
                                vmem_limit_bytes=VMEM_LIMIT)


def _const_spec(shape):
    return pl.BlockSpec(shape, lambda *_: (0,) * len(shape), pipeline_mode=pl.Buffered(1))


def _inproj_kernel(x_ref, g_ref, w_ref, cos_ref, sin_ref, glu_ref, qt_ref, k_ref, vt_ref):
    h = _rms(x_ref[0], g_ref[...]).astype(BF16)
    z = jnp.dot(h, w_ref[...], preferred_element_type=F32)
    glu_ref[0] = z[:, :CONV_CH] * jax.nn.sigmoid(z[:, CONV_CH:2 * CONV_CH])

    cos = cos_ref[...]
    sin = sin_ref[...]
    lane = lax.broadcasted_iota(jnp.int32, (TM, 128), 1)
    first_half = (lane % DIFF_DQ) < (DIFF_DQ // 2)

    def rope(t):
        partner = jnp.where(first_half, pltpu.roll(t, 96, 1), pltpu.roll(t, 32, 1))
        return t * cos + partner * sin

    q_off = 2 * CONV_CH
    k_off = q_off + DIFF_WIDTH
    v_off = k_off + DIFF_WIDTH
    for j in range(DIFF_HEADS):
        q = rope(z[:, q_off + 128 * j:q_off + 128 * (j + 1)]) * Q_SCALE
        qt_ref[0, j, 0] = q.T.astype(BF16)
        k = rope(z[:, k_off + 128 * j:k_off + 128 * (j + 1)])
        k_ref[0, :, 128 * j:128 * (j + 1)] = k.astype(BF16)
        v = z[:, v_off + 128 * j:v_off + 128 * (j + 1)]
        vt_ref[0, j, 0, 0:DIFF_DV, :] = v.T.astype(BF16)
        pad_row = lax.broadcasted_iota(jnp.int32, (V_ROWS - DIFF_DV, TM), 0)
        vt_ref[0, j, 0, DIFF_DV:V_ROWS, :] = jnp.where(pad_row == 0, 1.0, 0.0).astype(BF16)


def _inproj(x, g, w_in, cos, sin):
    b, s, _ = x.shape
    nt = s // TM
    return pl.pallas_call(
        _inproj_kernel,
        grid=(b, nt),
        in_specs=[
            pl.BlockSpec((1, TM, D_MODEL), lambda bi, i: (bi, i, 0)),
            _const_spec((1, D_MODEL)),
            _const_spec((D_MODEL, IN_COLS)),
            pl.BlockSpec((TM, 128), lambda bi, i: (i, 0)),
            pl.BlockSpec((TM, 128), lambda bi, i: (i, 0)),
        ],
        out_specs=[
            pl.BlockSpec((1, TM, CONV_CH), lambda bi, i: (bi, i, 0)),
            pl.BlockSpec((1, DIFF_HEADS, 1, 2 * DIFF_DQ, TM), lambda bi, i: (bi, 0, i, 0, 0)),
            pl.BlockSpec((1, TM, DIFF_WIDTH), lambda bi, i: (bi, i, 0)),
            pl.BlockSpec((1, DIFF_HEADS, 1, V_ROWS, TM), lambda bi, i: (bi, 0, i, 0, 0)),
        ],
        out_shape=[
            jax.ShapeDtypeStruct((b, s, CONV_CH), F32),
            jax.ShapeDtypeStruct((b, DIFF_HEADS, nt, 2 * DIFF_DQ, TM), BF16),
            jax.ShapeDtypeStruct((b, s, DIFF_WIDTH), BF16),
            jax.ShapeDtypeStruct((b, DIFF_HEADS, nt, V_ROWS, TM), BF16),
        ],
        compiler_params=_params(2),
        name="inproj",
    )(x, g, w_in, cos, sin)


def _conv_kernel(main_ref, prev_ref, next_ref, w_ref, b_ref, g_ref, o_ref, win_ref, sh_ref, y_ref):
    i = pl.program_id(1)
    n = pl.num_programs(1)
    prev = prev_ref[0]
    nxt = next_ref[0]
    win_ref[0:HALO, :] = jnp.where(i > 0, prev, jnp.zeros_like(prev))
    win_ref[HALO:HALO + TM, :] = main_ref[0]
    win_ref[HALO + TM:2 * HALO + TM, :] = jnp.where(i < n - 1, nxt, jnp.zeros_like(nxt))
    rows = TM + 2 * HALO
    lead = HALO - (CONV_WIDTH - 1) // 2
    for cb in range(CONV_CH // 128):
        lanes = slice(cb * 128, (cb + 1) * 128)
        wblk = win_ref[:, lanes]
        for r in range(1, 8):
            sh_ref[r - 1] = pltpu.roll(wblk, rows - r, 0)
        for rb in range(TM // CONV_ROWS):
            base = rb * CONV_ROWS
            acc = jnp.broadcast_to(b_ref[:, lanes], (CONV_ROWS, 128))
            for j in range(CONV_WIDTH):
                q, r = divmod(j + lead, 8)
                start = base + 8 * q
                if r == 0:
                    src = win_ref[start:start + CONV_ROWS, lanes]
                else:
                    src = sh_ref[r - 1, start:start + CONV_ROWS, :]
                acc = acc + src * w_ref[j:j + 1, lanes]
            y_ref[base:base + CONV_ROWS, lanes] = acc
    y = _rms(y_ref[...], g_ref[...])
    o_ref[0] = (y * jax.nn.sigmoid(y)).astype(BF16)


def _halo_specs(width, seq_len):
    per_tile = TM // HALO
    last = seq_len // HALO - 1

    def prev_map(bi, i):
        return (bi, jnp.maximum(i * per_tile - 1, 0), 0)

    def next_map(bi, i):
        return (bi, jnp.minimum((i + 1) * per_tile, last), 0)

    return (pl.BlockSpec((1, HALO, width), prev_map), pl.BlockSpec((1, HALO, width), next_map))


def _conv_mixer(glu, w, bias, g):
    b, s, _ = glu.shape
    prev_spec, next_spec = _halo_specs(CONV_CH, s)
    return pl.pallas_call(
        _conv_kernel,
        grid=(b, s // TM),
        in_specs=[
            pl.BlockSpec((1, TM, CONV_CH), lambda bi, i: (bi, i, 0)),
            prev_spec,
            next_spec,
            _const_spec((CONV_WIDTH + 1, CONV_CH)),
            _const_spec((1, CONV_CH)),
            _const_spec((1, CONV_CH)),
        ],
        out_specs=pl.BlockSpec((1, TM, CONV_CH), lambda bi, i: (bi, i, 0)),
        out_shape=jax.ShapeDtypeStruct((b, s, CONV_CH), BF16),
        scratch_shapes=[pltpu.VMEM((TM + 2 * HALO, CONV_CH), F32),
                        pltpu.VMEM((7, TM + 2 * HALO, 128), F32), pltpu.VMEM((TM, CONV_CH), F32)],
        compiler_params=_params(2),
        name="conv_mixer",
    )(glu, glu, glu, w, bias, g)


def _attn_kernel(qt_ref, k_ref, vt_ref, lq1_ref, lk1_ref, lq2_ref, lk2_ref, g_ref, o_ref,
                 s_ref, p_ref, acc_ref, *, n_chunks, n_qtiles):
    n_items = n_chunks * n_qtiles
    lg = n_chunks.bit_length() - 1
    tk = s_ref.shape[0]
    zeros = jnp.zeros((DIFF_DQ, TQ), BF16)
    lam = (jnp.exp(jnp.sum(lq1_ref[...] * lk1_ref[...], axis=-1, keepdims=True))
           - jnp.exp(jnp.sum(lq2_ref[...] * lk2_ref[...], axis=-1, keepdims=True)) + LAM_INIT)

    def stage1(t, m_prev):
        qi = t >> lg
        c = t & (n_chunks - 1)
        qt = qt_ref[0, 0, qi]
        qq = jnp.concatenate(
            [jnp.concatenate([qt[:DIFF_DQ], zeros], axis=0),
             jnp.concatenate([zeros, qt[DIFF_DQ:]], axis=0)], axis=1)
        off = pl.multiple_of(c * tk, tk)
        s = jnp.dot(k_ref[0, pl.ds(off, tk), :], qq, preferred_element_type=F32)
        s_ref[...] = s
        m_prev = jnp.where(c == 0, NEG_BIG, m_prev)
        m_new = jnp.maximum(m_prev, jnp.max(s, axis=0, keepdims=True))
        return m_new, jnp.exp2(m_prev - m_new)

    def stage2(m):
        p_ref[...] = jnp.exp2(s_ref[...] - m).astype(BF16)

    def stage3(t, a):
        c = t & (n_chunks - 1)
        pv = acc_ref[...] * a
        for i in range(tk // TM):
            pv = pv + jnp.dot(vt_ref[0, 0, (tk // TM) * c + i], p_ref[i * TM:(i + 1) * TM, :],
                              preferred_element_type=F32)
        acc_ref[...] = pv

    def finalize(qi):
        acc = acc_ref[...]
        o = acc[:DIFF_DV, :] * (1.0 / acc[DIFF_DV:DIFF_DV + 1, :])
        d = o[:, :TQ] - lam * o[:, TQ:]
        ms = jnp.mean(d * d, axis=0, keepdims=True)
        y = d * lax.rsqrt(ms + EPS) * g_ref[...] * (1.0 - LAM_INIT)
        o_ref[0, pl.ds(pl.multiple_of(qi * TQ, TQ), TQ), :] = y.T.astype(BF16)

    acc_ref[...] = jnp.zeros_like(acc_ref)
    m0, a0 = stage1(0, jnp.full((1, 2 * TQ), NEG_BIG, F32))
    stage2(m0)
    m1, a1 = stage1(1, m0)

    def step(t, carry, may_finalize):
        m_prev, a_prev1, a_prev2 = carry
        stage3(t - 2, a_prev2)
        stage2(m_prev)
        m_t, a_t = stage1(t, m_prev)
        if may_finalize:
            @pl.when(((t - 2) & (n_chunks - 1)) == n_chunks - 1)
            def _():
                finalize((t - 2) >> lg)

        return m_t, a_t, a_prev1

    def pair(j, carry):
        t = 2 * j + 2
        return step(t + 1, step(t, carry, False), True)

    m_last, a_last1, a_last2 = lax.fori_loop(0, (n_items - 2) // 2, pair, (m1, a1, a0))
    stage3(n_items - 2, a_last2)
    stage2(m_last)
    stage3(n_items - 1, a_last1)
    finalize(n_qtiles - 1)


def _diff_attention(qt, k, vt, lq1, lk1, lq2, lk2, g_col):
    b, s, _ = k.shape
    tk = min(TK, s // 2)
    n_chunks = s // tk
    n_qtiles = s // TQ
    assert n_chunks & (n_chunks - 1) == 0 and n_chunks >= 2 and tk % TM == 0
    return pl.pallas_call(
        functools.partial(_attn_kernel, n_chunks=n_chunks, n_qtiles=n_qtiles),
        grid=(b, DIFF_HEADS),
        in_specs=[
            pl.BlockSpec((1, 1, n_qtiles, 2 * DIFF_DQ, TQ), lambda bi, h: (bi, h, 0, 0, 0)),
            pl.BlockSpec((1, s, 2 * DIFF_DQ), lambda bi, h: (bi, 0, h)),
            pl.BlockSpec((1, 1, s // TM, V_ROWS, TM), lambda bi, h: (bi, h, 0, 0, 0)),
            _const_spec((1, DIFF_DQ)),
            _const_spec((1, DIFF_DQ)),
            _const_spec((1, DIFF_DQ)),
            _const_spec((1, DIFF_DQ)),
            _const_spec((DIFF_DV, 1)),
        ],
        out_specs=pl.BlockSpec((1, s, DIFF_DV), lambda bi, h: (bi, 0, h)),
        out_shape=jax.ShapeDtypeStruct((b, s, DIFF_WIDTH), BF16),
        scratch_shapes=[pltpu.VMEM((tk, 2 * TQ), F32), pltpu.VMEM((tk, 2 * TQ), BF16),
                        pltpu.VMEM((V_ROWS, 2 * TQ), F32)],
        compiler_params=_params(2),
        name="diff_attention",
    )(qt, k, vt, lq1, lk1, lq2, lk2, g_col)


def _memkv_kernel(m_ref, g_ref, w_ref, k_ref, v_ref):
    h = _rms(m_ref[0], g_ref[...]).astype(BF16)
    kv = jnp.dot(h, w_ref[...], preferred_element_type=F32)
    k_ref[0] = kv[:, :D_MODEL].astype(BF16)
    v_ref[0] = kv[:, D_MODEL:].astype(BF16)


def _mem_kv(mem, g, w_mkv):
    b = mem.shape[0]
    blk = pl.BlockSpec((1, N_MEM, D_MODEL), lambda bi: (bi, 0, 0))
    return pl.pallas_call(
        _memkv_kernel,
        grid=(b,),
        in_specs=[blk, _const_spec((1, D_MODEL)), _const_spec((D_MODEL, 2 * D_MODEL))],
        out_specs=[blk, blk],
        out_shape=[jax.ShapeDtypeStruct((b, N_MEM, D_MODEL), BF16)] * 2,
        compiler_params=_params(1),
        name="mem_kv",
    )(mem, g, w_mkv)


def _cross_kernel(x_ref, a_ref, b_ref, wout_ref, gc_ref, wq_ref, km_ref, vm_ref, wo_ref, gf_ref,
                  x2_ref, hf_ref):
    x1 = (x_ref[0]
          + jnp.dot(a_ref[0], wout_ref[:CONV_CH, :], preferred_element_type=F32)
          + jnp.dot(b_ref[0], wout_ref[CONV_CH:, :], preferred_element_type=F32))
    hq = _rms(x1, gc_ref[...]).astype(BF16)
    q = (jnp.dot(hq, wq_ref[...], preferred_element_type=F32) * (MEM_HD ** -0.5)).astype(BF16)
    heads = []
    for h in range(MEM_HEADS):
        sl = slice(h * MEM_HD, (h + 1) * MEM_HD)
        s = lax.dot_general(q[:, sl], km_ref[0, :, sl], (((1,), (1,)), ((), ())),
                            preferred_element_type=F32)
        p = jnp.exp(s - jnp.max(s, axis=-1, keepdims=True))
        l = jnp.sum(p, axis=-1, keepdims=True)
        o = jnp.dot(p.astype(BF16), vm_ref[0, :, sl], preferred_element_type=F32)
        heads.append((o * (1.0 / l)).astype(BF16))
    o_all = jnp.concatenate(heads, axis=-1)
    x2 = x1 + jnp.dot(o_all, wo_ref[...], preferred_element_type=F32)
    x2_ref[0] = x2
    hf_ref[0] = _rms(x2, gf_ref[...]).astype(BF16)


def _cross(x, a, bb, w_out, g_cross, w_mq, k_mem, v_mem, w_mo, g_ffn):
    b, s, _ = x.shape
    tile = lambda width: pl.BlockSpec((1, TM, width), lambda bi, i: (bi, i, 0))
    mem_blk = pl.BlockSpec((1, N_MEM, D_MODEL), lambda bi, i: (bi, 0, 0))
    return pl.pallas_call(
        _cross_kernel,
        grid=(b, s // TM),
        in_specs=[
            tile(D_MODEL), tile(CONV_CH), tile(DIFF_WIDTH),
            _const_spec((D_MODEL, D_MODEL)), _const_spec((1, D_MODEL)),
            _const_spec((D_MODEL, D_MODEL)), mem_blk, mem_blk,
            _const_spec((D_MODEL, D_MODEL)), _const_spec((1, D_MODEL)),
        ],
        out_specs=[tile(D_MODEL), tile(D_MODEL)],
        out_shape=[jax.ShapeDtypeStruct((b, s, D_MODEL), F32),
                   jax.ShapeDtypeStruct((b, s, D_MODEL), BF16)],
        compiler_params=_params(2),
        name="cross_attention",
    )(x, a, bb, w_out, g_cross, w_mq, k_mem, v_mem, w_mo, g_ffn)


def _ffn_kernel(h_ref, hprev_ref, hnext_ref, x2_ref, wup_ref, dw_ref, db_ref, wdown_ref, gfin_ref,
                y_ref, gate_ref):
    i = pl.program_id(1)
    n = pl.num_programs(1)
    prev = hprev_ref[0]
    nxt = hnext_ref[0]
    hwin = jnp.concatenate(
        [jnp.where(i > 0, prev, jnp.zeros_like(prev)), h_ref[0],
         jnp.where(i < n - 1, nxt, jnp.zeros_like(nxt))], axis=0)

    def conv3(col0):
        cols = slice(col0, col0 + FF_CHUNK)
        u = jnp.dot(hwin, wup_ref[:, cols], preferred_element_type=F32)
        return (pltpu.roll(u, 1, 0)[HALO:HALO + TM] * dw_ref[0:1, cols]
                + u[HALO:HALO + TM] * dw_ref[1:2, cols]
                + pltpu.roll(u, TM + 2 * HALO - 1, 0)[HALO:HALO + TM] * dw_ref[2:3, cols]
                + db_ref[:, cols])

    for c in range(D_FF // FF_CHUNK):
        val = conv3(c * FF_CHUNK)
        gate = conv3(D_FF + c * FF_CHUNK)
        gate_ref[:, c * FF_CHUNK:(c + 1) * FF_CHUNK] = (gate * jax.nn.sigmoid(gate) * val).astype(BF16)
    x3 = x2_ref[0] + jnp.dot(gate_ref[...], wdown_ref[...], preferred_element_type=F32)
    y_ref[0] = _rms(x3, gfin_ref[...])


def _ffn(hf, x2, w_up, dw_w, dw_b, w_down, g_final):
    b, s, _ = x2.shape
    tile = pl.BlockSpec((1, TM, D_MODEL), lambda bi, i: (bi, i, 0))
    prev_spec, next_spec = _halo_specs(D_MODEL, s)
    return pl.pallas_call(
        _ffn_kernel,
        grid=(b, s // TM),
        in_specs=[
            tile, prev_spec, next_spec, tile,
            _const_spec((D_MODEL, 2 * D_FF)), _const_spec((8, 2 * D_FF)), _const_spec((1, 2 * D_FF)),
            _const_spec((D_FF, D_MODEL)), _const_spec((1, D_MODEL)),
        ],
        out_specs=tile,
        out_shape=jax.ShapeDtypeStruct((b, s, D_MODEL), F32),
        scratch_shapes=[pltpu.VMEM((TM, D_FF), BF16)],
        compiler_params=_params(2),
        name="conv_ffn",
    )(hf, hf, hf, x2, w_up, dw_w, dw_b, w_down, g_final)


def _rope_tables(s):
    inv = ROPE_THETA ** (-jnp.arange(0, DIFF_DQ, 2, dtype=F32) / DIFF_DQ)
    ang = jnp.arange(s, dtype=F32)[:, None] * inv[None, :]
    c, sn = jnp.cos(ang), jnp.sin(ang)
    return jnp.concatenate([c, c, c, c], axis=1), jnp.concatenate([-sn, sn, -sn, sn], axis=1)


def _encoder(x, mem, p):
    cos, sin = _rope_tables(x.shape[1])
    glu, qt, k, vt = _inproj(x, p["norm_mix"], p["w_in"], cos, sin)
    a = _conv_mixer(glu, p["conv_dw_w"], p["conv_dw_b"], p["conv_norm"])
    bb = _diff_attention(qt, k, vt, p["lq1"], p["lk1"], p["lq2"], p["lk2"], p["diff_subln"])
    k_mem, v_mem = _mem_kv(mem, p["norm_mem"], p["w_mkv"])
    x2, hf = _cross(x, a, bb, p["w_out"], p["norm_cross"], p["w_mq"], k_mem, v_mem, p["w_mo"],
                    p["norm_ffn"])
    return _ffn(hf, x2, p["w_up"], p["ffn_dw_w"], p["ffn_dw_b"], p["w_down"], p["norm_final"])


def kernel(x_prompt, x_sample, mem_prompt, mem_sample, norm_mix, w_in, conv_dw_w, conv_dw_b, conv_norm, lambda_q1, lambda_k1, lambda_q2, lambda_k2, diff_subln, w_out, norm_cross, norm_mem, w_mq, w_mkv, w_mo, norm_ffn, w_up, ffn_dw_w, ffn_dw_b, w_down, norm_final):
    row = lambda v: v.reshape(1, -1).astype(F32)
    p = {
        "norm_mix": row(norm_mix[0]),
        "w_in": w_in[0].astype(BF16),
        "conv_dw_w": jnp.pad(conv_dw_w[0], ((0, 1), (0, 0))),
        "conv_dw_b": row(conv_dw_b[0]),
        "conv_norm": row(conv_norm[0]),
        "lq1": row(lambda_q1[0]), "lk1": row(lambda_k1[0]),
        "lq2": row(lambda_q2[0]), "lk2": row(lambda_k2[0]),
        "diff_subln": diff_subln[0].reshape(-1, 1).astype(F32),
        "w_out": w_out[0].astype(BF16),
        "norm_cross": row(norm_cross[0]),
        "norm_mem": row(norm_mem[0]),
        "w_mq": w_mq[0].astype(BF16),
        "w_mkv": w_mkv[0].astype(BF16),
        "w_mo": w_mo[0].astype(BF16),
        "norm_ffn": row(norm_ffn[0]),
        "w_up": w_up[0].astype(BF16),
        "ffn_dw_w": jnp.pad(ffn_dw_w[0], ((0, 5), (0, 0))),
        "ffn_dw_b": row(ffn_dw_b[0]),
        "w_down": w_down[0].astype(BF16),
        "norm_final": row(norm_final),
    }
    return (_encoder(x_prompt, mem_prompt, p), _encoder(x_sample, mem_sample, p))
```

```python
import functools
import math

import jax
import jax.numpy as jnp
from jax import lax
from jax.experimental import pallas as pl
from jax.experimental.pallas import tpu as pltpu

F32 = jnp.float32
BF16 = jnp.bfloat16

D_MODEL = 1024
N_MEM = 256
CONV_CH = 512
CONV_WIDTH = 31
DIFF_HEADS = 4
DIFF_DQ = 64
DIFF_DV = 128
DIFF_WIDTH = DIFF_HEADS * DIFF_DV
IN_COLS = 2 * CONV_CH + 3 * DIFF_WIDTH
MEM_HEADS = 4
MEM_HD = D_MODEL // MEM_HEADS
D_FF = 2816
ROPE_THETA = 10000.0
EPS = 1e-6
LAM_INIT = 0.8 - 0.6 * math.exp(-0.3 * 0)
Q_SCALE = DIFF_DQ ** -0.5 * math.log2(math.e)

TM = 512
TQ = 512
ATTN_TOKENS = 16384
TK = 2048
V_ROWS = DIFF_DV + 16
HALO = 16
CONV_ROWS = 64
FF_CHUNK = 256
VMEM_LIMIT = 56 * 1024 * 1024
NEG_BIG = -1e30


def _rms(x, g):
    ms = jnp.mean(x * x, axis=-1, keepdims=True)
    return x * lax.rsqrt(ms + EPS) * g


def _params(n_axes):
    return pltpu.CompilerParams(dimension_semantics=("arbitrary",) * n_axes,
                                vmem_limit_bytes=VMEM_LIMIT)


def _const_spec(shape):
    return pl.BlockSpec(shape, lambda *_: (0,) * len(shape), pipeline_mode=pl.Buffered(1))


def _inproj_kernel(x_ref, g_ref, w_ref, cos_ref, sin_ref, glu_ref, qt_ref, k_ref, vt_ref):
    h = _rms(x_ref[0], g_ref[...]).astype(BF16)
    z = jnp.dot(h, w_ref[...], preferred_element_type=F32)
    glu_ref[0] = z[:, :CONV_CH] * jax.nn.sigmoid(z[:, CONV_CH:2 * CONV_CH])

    cos = cos_ref[...]
    sin = sin_ref[...]
    lane = lax.broadcasted_iota(jnp.int32, (TM, 128), 1)
    first_half = (lane % DIFF_DQ) < (DIFF_DQ // 2)

    def rope(t):
        partner = jnp.where(first_half, pltpu.roll(t, 96, 1), pltpu.roll(t, 32, 1))
        return t * cos + partner * sin

    q_off = 2 * CONV_CH
    k_off = q_off + DIFF_WIDTH
    v_off = k_off + DIFF_WIDTH
    for j in range(DIFF_HEADS):
        q = rope(z[:, q_off + 128 * j:q_off + 128 * (j + 1)]) * Q_SCALE
        qt_ref[0, j, 0] = q.T.astype(BF16)
        k = rope(z[:, k_off + 128 * j:k_off + 128 * (j + 1)])
        k_ref[0, :, 128 * j:128 * (j + 1)] = k.astype(BF16)
        v = z[:, v_off + 128 * j:v_off + 128 * (j + 1)]
        vt_ref[0, j, 0, 0:DIFF_DV, :] = v.T.astype(BF16)
        pad_row = lax.broadcasted_iota(jnp.int32, (V_ROWS - DIFF_DV, TM), 0)
        vt_ref[0, j, 0, DIFF_DV:V_ROWS, :] = jnp.where(pad_row == 0, 1.0, 0.0).astype(BF16)


def _inproj(x, g, w_in, cos, sin):
    b, s, _ = x.shape
    nt = s // TM
    return pl.pallas_call(
        _inproj_kernel,
        grid=(b, nt),
        in_specs=[
            pl.BlockSpec((1, TM, D_MODEL), lambda bi, i: (bi, i, 0)),
            _const_spec((1, D_MODEL)),
            _const_spec((D_MODEL, IN_COLS)),
            pl.BlockSpec((TM, 128), lambda bi, i: (i, 0)),
            pl.BlockSpec((TM, 128), lambda bi, i: (i, 0)),
        ],
        out_specs=[
            pl.BlockSpec((1, TM, CONV_CH), lambda bi, i: (bi, i, 0)),
            pl.BlockSpec((1, DIFF_HEADS, 1, 2 * DIFF_DQ, TM), lambda bi, i: (bi, 0, i, 0, 0)),
            pl.BlockSpec((1, TM, DIFF_WIDTH), lambda bi, i: (bi, i, 0)),
            pl.BlockSpec((1, DIFF_HEADS, 1, V_ROWS, TM), lambda bi, i: (bi, 0, i, 0, 0)),
        ],
        out_shape=[
            jax.ShapeDtypeStruct((b, s, CONV_CH), F32),
            jax.ShapeDtypeStruct((b, DIFF_HEADS, nt, 2 * DIFF_DQ, TM), BF16),
            jax.ShapeDtypeStruct((b, s, DIFF_WIDTH), BF16),
            jax.ShapeDtypeStruct((b, DIFF_HEADS, nt, V_ROWS, TM), BF16),
        ],
        compiler_params=_params(2),
        name="inproj",
    )(x, g, w_in, cos, sin)


def _conv_kernel(main_ref, prev_ref, next_ref, w_ref, b_ref, g_ref, o_ref, win_ref, sh_ref, y_ref):
    i = pl.program_id(1)
    n = pl.num_programs(1)
    prev = prev_ref[0]
    nxt = next_ref[0]
    win_ref[0:HALO, :] = jnp.where(i > 0, prev, jnp.zeros_like(prev))
    win_ref[HALO:HALO + TM, :] = main_ref[0]
    win_ref[HALO + TM:2 * HALO + TM, :] = jnp.where(i < n - 1, nxt, jnp.zeros_like(nxt))
    rows = TM + 2 * HALO
    lead = HALO - (CONV_WIDTH - 1) // 2
    for cb in range(CONV_CH // 128):
        lanes = slice(cb * 128, (cb + 1) * 128)
        wblk = win_ref[:, lanes]
        for r in range(1, 8):
            sh_ref[r - 1] = pltpu.roll(wblk, rows - r, 0)
        for rb in range(TM // CONV_ROWS):
            base = rb * CONV_ROWS
            acc = jnp.broadcast_to(b_ref[:, lanes], (CONV_ROWS, 128))
            for j in range(CONV_WIDTH):
                q, r = divmod(j + lead, 8)
                start = base + 8 * q
                if r == 0:
                    src = win_ref[start:start + CONV_ROWS, lanes]
                else:
                    src = sh_ref[r - 1, start:start + CONV_ROWS, :]
                acc = acc + src * w_ref[j:j + 1, lanes]
            y_ref[base:base + CONV_ROWS, lanes] = acc
    y = _rms(y_ref[...], g_ref[...])
    o_ref[0] = (y * jax.nn.sigmoid(y)).astype(BF16)


def _halo_specs(width, seq_len):
    per_tile = TM // HALO
    last = seq_len // HALO - 1

    def prev_map(bi, i):
        return (bi, jnp.maximum(i * per_tile - 1, 0), 0)

    def next_map(bi, i):
        return (bi, jnp.minimum((i + 1) * per_tile, last), 0)

    return (pl.BlockSpec((1, HALO, width), prev_map), pl.BlockSpec((1, HALO, width), next_map))


def _conv_mixer(glu, w, bias, g):
    b, s, _ = glu.shape
    prev_spec, next_spec = _halo_specs(CONV_CH, s)
    return pl.pallas_call(
        _conv_kernel,
        grid=(b, s // TM),
        in_specs=[
            pl.BlockSpec((1, TM, CONV_CH), lambda bi, i: (bi, i, 0)),
            prev_spec,
            next_spec,
            _const_spec((CONV_WIDTH + 1, CONV_CH)),
            _const_spec((1, CONV_CH)),
            _const_spec((1, CONV_CH)),
        ],
        out_specs=pl.BlockSpec((1, TM, CONV_CH), lambda bi, i: (bi, i, 0)),
        out_shape=jax.ShapeDtypeStruct((b, s, CONV_CH), BF16),
        scratch_shapes=[pltpu.VMEM((TM + 2 * HALO, CONV_CH), F32),
                        pltpu.VMEM((7, TM + 2 * HALO, 128), F32), pltpu.VMEM((TM, CONV_CH), F32)],
        compiler_params=_params(2),
        name="conv_mixer",
    )(glu, glu, glu, w, bias, g)


def _attn_kernel(qt_ref, k_ref, vt_ref, lq1_ref, lk1_ref, lq2_ref, lk2_ref, g_ref, o_ref,
                 s_ref, p_ref, acc_ref, *, n_chunks, n_qtiles, n_batch):
    n_items = n_batch * n_qtiles * n_chunks
    lg = n_chunks.bit_length() - 1
    lgq = n_qtiles.bit_length() - 1
    tk = s_ref.shape[0]
    zeros = jnp.zeros((DIFF_DQ, TQ), BF16)
    lam = (jnp.exp(jnp.sum(lq1_ref[...] * lk1_ref[...], axis=-1, keepdims=True))
           - jnp.exp(jnp.sum(lq2_ref[...] * lk2_ref[...], axis=-1, keepdims=True)) + LAM_INIT)

    def stage1(t, m_prev):
        bb = t >> (lg + lgq)
        qi = (t >> lg) & (n_qtiles - 1)
        c = t & (n_chunks - 1)
        qt = qt_ref[bb, 0, qi]
        qq = jnp.concatenate(
            [jnp.concatenate([qt[:DIFF_DQ], zeros], axis=0),
             jnp.concatenate([zeros, qt[DIFF_DQ:]], axis=0)], axis=1)
        off = pl.multiple_of(c * tk, tk)
        s = jnp.dot(k_ref[bb, pl.ds(off, tk), :], qq, preferred_element_type=F32)
        s_ref[...] = s
        m_prev = jnp.where(c == 0, NEG_BIG, m_prev)
        m_new = jnp.maximum(m_prev, jnp.max(s, axis=0, keepdims=True))
        return m_new, jnp.exp2(m_prev - m_new)

    def stage2(m):
        p_ref[...] = jnp.exp2(s_ref[...] - m).astype(BF16)

    def stage3(t, a):
        bb = t >> (lg + lgq)
        c = t & (n_chunks - 1)
        pv = acc_ref[...] * a
        for i in range(tk // TM):
            pv = pv + jnp.dot(vt_ref[bb, 0, (tk // TM) * c + i], p_ref[i * TM:(i + 1) * TM, :],
                              preferred_element_type=F32)
        acc_ref[...] = pv

    def finalize(tq):
        bb = tq >> lgq
        qi = tq & (n_qtiles - 1)
        acc = acc_ref[...]
        o = acc[:DIFF_DV, :] * (1.0 / acc[DIFF_DV:DIFF_DV + 1, :])
        d = o[:, :TQ] - lam * o[:, TQ:]
        ms = jnp.mean(d * d, axis=0, keepdims=True)
        y = d * lax.rsqrt(ms + EPS) * g_ref[...] * (1.0 - LAM_INIT)
        o_ref[bb, pl.ds(pl.multiple_of(qi * TQ, TQ), TQ), :] = y.T.astype(BF16)

    acc_ref[...] = jnp.zeros_like(acc_ref)
    m0, a0 = stage1(0, jnp.full((1, 2 * TQ), NEG_BIG, F32))
    stage2(m0)
    m1, a1 = stage1(1, m0)

    def step(t, carry, may_finalize):
        m_prev, a_prev1, a_prev2 = carry
        stage3(t - 2, a_prev2)
        stage2(m_prev)
        m_t, a_t = stage1(t, m_prev)
        if may_finalize:
            @pl.when(((t - 2) & (n_chunks - 1)) == n_chunks - 1)
            def _():
                finalize((t - 2) >> lg)

        return m_t, a_t, a_prev1

    def pair(j, carry):
        t = 2 * j + 2
        return step(t + 1, step(t, carry, False), True)

    m_last, a_last1, a_last2 = lax.fori_loop(0, (n_items - 2) // 2, pair, (m1, a1, a0))
    stage3(n_items - 2, a_last2)
    stage2(m_last)
    stage3(n_items - 1, a_last1)
    finalize(n_batch * n_qtiles - 1)


def _diff_attention(qt, k, vt, lq1, lk1, lq2, lk2, g_col):
    b, s, _ = k.shape
    tk = min(TK, s // 2)
    n_chunks = s // tk
    n_qtiles = s // TQ
    nb = max(1, min(b, ATTN_TOKENS // s))
    assert n_chunks & (n_chunks - 1) == 0 and n_chunks >= 2 and tk % TM == 0
    assert n_qtiles & (n_qtiles - 1) == 0 and b % nb == 0
    return pl.pallas_call(
        functools.partial(_attn_kernel, n_chunks=n_chunks, n_qtiles=n_qtiles, n_batch=nb),
        grid=(b // nb, DIFF_HEADS),
        in_specs=[
            pl.BlockSpec((nb, 1, n_qtiles, 2 * DIFF_DQ, TQ), lambda bi, h: (bi, h, 0, 0, 0)),
            pl.BlockSpec((nb, s, 2 * DIFF_DQ), lambda bi, h: (bi, 0, h)),
            pl.BlockSpec((nb, 1, s // TM, V_ROWS, TM), lambda bi, h: (bi, h, 0, 0, 0)),
            _const_spec((1, DIFF_DQ)),
            _const_spec((1, DIFF_DQ)),
            _const_spec((1, DIFF_DQ)),
            _const_spec((1, DIFF_DQ)),
            _const_spec((DIFF_DV, 1)),
        ],
        out_specs=pl.BlockSpec((nb, s, DIFF_DV), lambda bi, h: (bi, 0, h)),
        out_shape=jax.ShapeDtypeStruct((b, s, DIFF_WIDTH), BF16),
        scratch_shapes=[pltpu.VMEM((tk, 2 * TQ), F32), pltpu.VMEM((tk, 2 * TQ), BF16),
                        pltpu.VMEM((V_ROWS, 2 * TQ), F32)],
        compiler_params=_params(2),
        name="diff_attention",
    )(qt, k, vt, lq1, lk1, lq2, lk2, g_col)


def _memkv_kernel(m_ref, g_ref, w_ref, k_ref, v_ref):
    h = _rms(m_ref[0], g_ref[...]).astype(BF16)
    kv = jnp.dot(h, w_ref[...], preferred_element_type=F32)
    k_ref[0] = kv[:, :D_MODEL].astype(BF16)
    v_ref[0] = kv[:, D_MODEL:].astype(BF16)


def _mem_kv(mem, g, w_mkv):
    b = mem.shape[0]
    blk = pl.BlockSpec((1, N_MEM, D_MODEL), lambda bi: (bi, 0, 0))
    return pl.pallas_call(
        _memkv_kernel,
        grid=(b,),
        in_specs=[blk, _const_spec((1, D_MODEL)), _const_spec((D_MODEL, 2 * D_MODEL))],
        out_specs=[blk, blk],
        out_shape=[jax.ShapeDtypeStruct((b, N_MEM, D_MODEL), BF16)] * 2,
        compiler_params=_params(1),
        name="mem_kv",
    )(mem, g, w_mkv)


def _cross_kernel(x_ref, a_ref, b_ref, wout_ref, gc_ref, wq_ref, km_ref, vm_ref, wo_ref, gf_ref,
                  x2_ref, hf_ref):
    x1 = (x_ref[0]
          + jnp.dot(a_ref[0], wout_ref[:CONV_CH, :], preferred_element_type=F32)
          + jnp.dot(b_ref[0], wout_ref[CONV_CH:, :], preferred_element_type=F32))
    hq = _rms(x1, gc_ref[...]).astype(BF16)
    q = (jnp.dot(hq, wq_ref[...], preferred_element_type=F32) * (MEM_HD ** -0.5)).astype(BF16)
    heads = []
    for h in range(MEM_HEADS):
        sl = slice(h * MEM_HD, (h + 1) * MEM_HD)
        s = lax.dot_general(q[:, sl], km_ref[0, :, sl], (((1,), (1,)), ((), ())),
                            preferred_element_type=F32)
        p = jnp.exp(s - jnp.max(s, axis=-1, keepdims=True))
        l = jnp.sum(p, axis=-1, keepdims=True)
        o = jnp.dot(p.astype(BF16), vm_ref[0, :, sl], preferred_element_type=F32)
        heads.append((o * (1.0 / l)).astype(BF16))
    o_all = jnp.concatenate(heads, axis=-1)
    x2 = x1 + jnp.dot(o_all, wo_ref[...], preferred_element_type=F32)
    x2_ref[0] = x2
    hf_ref[0] = _rms(x2, gf_ref[...]).astype(BF16)


def _cross(x, a, bb, w_out, g_cross, w_mq, k_mem, v_mem, w_mo, g_ffn):
    b, s, _ = x.shape
    tile = lambda width: pl.BlockSpec((1, TM, width), lambda bi, i: (bi, i, 0))
    mem_blk = pl.BlockSpec((1, N_MEM, D_MODEL), lambda bi, i: (bi, 0, 0))
    return pl.pallas_call(
        _cross_kernel,
        grid=(b, s // TM),
        in_specs=[
            tile(D_MODEL), tile(CONV_CH), tile(DIFF_WIDTH),
            _const_spec((D_MODEL, D_MODEL)), _const_spec((1, D_MODEL)),
            _const_spec((D_MODEL, D_MODEL)), mem_blk, mem_blk,
            _const_spec((D_MODEL, D_MODEL)), _const_spec((1, D_MODEL)),
        ],
        out_specs=[tile(D_MODEL), tile(D_MODEL)],
        out_shape=[jax.ShapeDtypeStruct((b, s, D_MODEL), F32),
                   jax.ShapeDtypeStruct((b, s, D_MODEL), BF16)],
        compiler_params=_params(2),
        name="cross_attention",
    )(x, a, bb, w_out, g_cross, w_mq, k_mem, v_mem, w_mo, g_ffn)


def _ffn_kernel(h_ref, hprev_ref, hnext_ref, x2_ref, wup_ref, dw_ref, db_ref, wdown_ref, gfin_ref,
                y_ref, gate_ref):
    i = pl.program_id(1)
    n = pl.num_programs(1)
    prev = hprev_ref[0]
    nxt = hnext_ref[0]
    hwin = jnp.concatenate(
        [jnp.where(i > 0, prev, jnp.zeros_like(prev)), h_ref[0],
         jnp.where(i < n - 1, nxt, jnp.zeros_like(nxt))], axis=0)

    def conv3(col0):
        cols = slice(col0, col0 + FF_CHUNK)
        u = jnp.dot(hwin, wup_ref[:, cols], preferred_element_type=F32)
        return (pltpu.roll(u, 1, 0)[HALO:HALO + TM] * dw_ref[0:1, cols]
                + u[HALO:HALO + TM] * dw_ref[1:2, cols]
                + pltpu.roll(u, TM + 2 * HALO - 1, 0)[HALO:HALO + TM] * dw_ref[2:3, cols]
                + db_ref[:, cols])

    for c in range(D_FF // FF_CHUNK):
        val = conv3(c * FF_CHUNK)
        gate = conv3(D_FF + c * FF_CHUNK)
        gate_ref[:, c * FF_CHUNK:(c + 1) * FF_CHUNK] = (gate * jax.nn.sigmoid(gate) * val).astype(BF16)
    x3 = x2_ref[0] + jnp.dot(gate_ref[...], wdown_ref[...], preferred_element_type=F32)
    y_ref[0] = _rms(x3, gfin_ref[...])


def _ffn(hf, x2, w_up, dw_w, dw_b, w_down, g_final):
    b, s, _ = x2.shape
    tile = pl.BlockSpec((1, TM, D_MODEL), lambda bi, i: (bi, i, 0))
    prev_spec, next_spec = _halo_specs(D_MODEL, s)
    return pl.pallas_call(
        _ffn_kernel,
        grid=(b, s // TM),
        in_specs=[
            tile, prev_spec, next_spec, tile,
            _const_spec((D_MODEL, 2 * D_FF)), _const_spec((8, 2 * D_FF)), _const_spec((1, 2 * D_FF)),
            _const_spec((D_FF, D_MODEL)), _const_spec((1, D_MODEL)),
        ],
        out_specs=tile,
        out_shape=jax.ShapeDtypeStruct((b, s, D_MODEL), F32),
        scratch_shapes=[pltpu.VMEM((TM, D_FF), BF16)],
        compiler_params=_params(2),
        name="conv_ffn",
    )(hf, hf, hf, x2, w_up, dw_w, dw_b, w_down, g_final)


def _rope_tables(s):
    inv = ROPE_THETA ** (-jnp.arange(0, DIFF_DQ, 2, dtype=F32) / DIFF_DQ)
    ang = jnp.arange(s, dtype=F32)[:, None] * inv[None, :]
    c, sn = jnp.cos(ang), jnp.sin(ang)
    return jnp.concatenate([c, c, c, c], axis=1), jnp.concatenate([-sn, sn, -sn, sn], axis=1)


def _encoder(x, mem, p):
    cos, sin = _rope_tables(x.shape[1])
    glu, qt, k, vt = _inproj(x, p["norm_mix"], p["w_in"], cos, sin)
    a = _conv_mixer(glu, p["conv_dw_w"], p["conv_dw_b"], p["conv_norm"])
    bb = _diff_attention(qt, k, vt, p["lq1"], p["lk1"], p["lq2"], p["lk2"], p["diff_subln"])
    k_mem, v_mem = _mem_kv(mem, p["norm_mem"], p["w_mkv"])
    x2, hf = _cross(x, a, bb, p["w_out"], p["norm_cross"], p["w_mq"], k_mem, v_mem, p["w_mo"],
                    p["norm_ffn"])
    return _ffn(hf, x2, p["w_up"], p["ffn_dw_w"], p["ffn_dw_b"], p["w_down"], p["norm_final"])


def kernel(x_prompt, x_sample, mem_prompt, mem_sample, norm_mix, w_in, conv_dw_w, conv_dw_b, conv_norm, lambda_q1, lambda_k1, lambda_q2, lambda_k2, diff_subln, w_out, norm_cross, norm_mem, w_mq, w_mkv, w_mo, norm_ffn, w_up, ffn_dw_w, ffn_dw_b, w_down, norm_final):
    row = lambda v: v.reshape(1, -1).astype(F32)
    p = {
        "norm_mix": row(norm_mix[0]),
        "w_in": w_in[0].astype(BF16),
        "conv_dw_w": jnp.pad(conv_dw_w[0], ((0, 1), (0, 0))),
        "conv_dw_b": row(conv_dw_b[0]),
        "conv_norm": row(conv_norm[0]),
        "lq1": row(lambda_q1[0]), "lk1": row(lambda_k1[0]),
        "lq2": row(lambda_q2[0]), "lk2": row(lambda_k2[0]),
        "diff_subln": diff_subln[0].reshape(-1, 1).astype(F32),
        "w_out": w_out[0].astype(BF16),
        "norm_cross": row(norm_cross[0]),
        "norm_mem": row(norm_mem[0]),
        "w_mq": w_mq[0].astype(BF16),
        "w_mkv": w_mkv[0].astype(BF16),
        "w_mo": w_mo[0].astype(BF16),
        "norm_ffn": row(norm_ffn[0]),
        "w_up": w_up[0].astype(BF16),
        "ffn_dw_w": jnp.pad(ffn_dw_w[0], ((0, 5), (0, 0))),
        "ffn_dw_b": row(ffn_dw_b[0]),
        "w_down": w_down[0].astype(BF16),
        "norm_final": row(norm_final),
    }
    return (_encoder(x_prompt, mem_prompt, p), _encoder(x_sample, mem_sample, p))
```

```python
import functools
import math

import jax
import jax.numpy as jnp
from jax import lax
from jax.experimental import pallas as pl
from jax.experimental.pallas import tpu as pltpu

F32 = jnp.float32
BF16 = jnp.bfloat16

D_MODEL = 1024
N_MEM = 256
CONV_CH = 512
CONV_WIDTH = 31
DIFF_HEADS = 4
DIFF_DQ = 64
DIFF_DV = 128
DIFF_WIDTH = DIFF_HEADS * DIFF_DV
IN_COLS = 2 * CONV_CH + 3 * DIFF_WIDTH
MEM_HEADS = 4
MEM_HD = D_MODEL // MEM_HEADS
D_FF = 2816
ROPE_THETA = 10000.0
EPS = 1e-6
LAM_INIT = 0.8 - 0.6 * math.exp(-0.3 * 0)
Q_SCALE = DIFF_DQ ** -0.5 * math.log2(math.e)

TM = 512
TC = 1024
TF = 1024
TQ = 512
TK = 2048
V_ROWS = DIFF_DV + 16
HALO = 16
CONV_ROWS = 64
FF_CHUNK = 256
VMEM_LIMIT = 56 * 1024 * 1024
NEG_BIG = -1e30


def _rms(x, g):
    ms = jnp.mean(x * x, axis=-1, keepdims=True)
    return x * lax.rsqrt(ms + EPS) * g


def _params(n_axes):
    return pltpu.CompilerParams(dimension_semantics=("arbitrary",) * n_axes,
                                vmem_limit_bytes=VMEM_LIMIT)


def _const_spec(shape):
    return pl.BlockSpec(shape, lambda *_: (0,) * len(shape), pipeline_mode=pl.Buffered(1))


def _inproj_kernel(x_ref, g_ref, w_ref, cos_ref, sin_ref, glu_ref, qt_ref, k_ref, vt_ref):
    h = _rms(x_ref[0], g_ref[...]).astype(BF16)
    z = jnp.dot(h, w_ref[...], preferred_element_type=F32)
    glu_ref[0] = z[:, :CONV_CH] * jax.nn.sigmoid(z[:, CONV_CH:2 * CONV_CH])

    cos = cos_ref[...]
    sin = sin_ref[...]
    lane = lax.broadcasted_iota(jnp.int32, (TM, 128), 1)
    first_half = (lane % DIFF_DQ) < (DIFF_DQ // 2)

    def rope(t):
        partner = jnp.where(first_half, pltpu.roll(t, 96, 1), pltpu.roll(t, 32, 1))
        return t * cos + partner * sin

    q_off = 2 * CONV_CH
    k_off = q_off + DIFF_WIDTH
    v_off = k_off + DIFF_WIDTH
    for j in range(DIFF_HEADS):
        q = rope(z[:, q_off + 128 * j:q_off + 128 * (j + 1)]) * Q_SCALE
        qt_ref[0, j, 0] = q.T.astype(BF16)
        k = rope(z[:, k_off + 128 * j:k_off + 128 * (j + 1)])
        k_ref[0, :, 128 * j:128 * (j + 1)] = k.astype(BF16)
        v = z[:, v_off + 128 * j:v_off + 128 * (j + 1)]
        vt_ref[0, j, 0, 0:DIFF_DV, :] = v.T.astype(BF16)
        pad_row = lax.broadcasted_iota(jnp.int32, (V_ROWS - DIFF_DV, TM), 0)
        vt_ref[0, j, 0, DIFF_DV:V_ROWS, :] = jnp.where(pad_row == 0, 1.0, 0.0).astype(BF16)


def _inproj(x, g, w_in, cos, sin):
    b, s, _ = x.shape
    nt = s // TM
    return pl.pallas_call(
        _inproj_kernel,
        grid=(b, nt),
        in_specs=[
            pl.BlockSpec((1, TM, D_MODEL), lambda bi, i: (bi, i, 0)),
            _const_spec((1, D_MODEL)),
            _const_spec((D_MODEL, IN_COLS)),
            pl.BlockSpec((TM, 128), lambda bi, i: (i, 0)),
            pl.BlockSpec((TM, 128), lambda bi, i: (i, 0)),
        ],
        out_specs=[
            pl.BlockSpec((1, TM, CONV_CH), lambda bi, i: (bi, i, 0)),
            pl.BlockSpec((1, DIFF_HEADS, 1, 2 * DIFF_DQ, TM), lambda bi, i: (bi, 0, i, 0, 0)),
            pl.BlockSpec((1, TM, DIFF_WIDTH), lambda bi, i: (bi, i, 0)),
            pl.BlockSpec((1, DIFF_HEADS, 1, V_ROWS, TM), lambda bi, i: (bi, 0, i, 0, 0)),
        ],
        out_shape=[
            jax.ShapeDtypeStruct((b, s, CONV_CH), F32),
            jax.ShapeDtypeStruct((b, DIFF_HEADS, nt, 2 * DIFF_DQ, TM), BF16),
            jax.ShapeDtypeStruct((b, s, DIFF_WIDTH), BF16),
            jax.ShapeDtypeStruct((b, DIFF_HEADS, nt, V_ROWS, TM), BF16),
        ],
        compiler_params=_params(2),
        name="inproj",
    )(x, g, w_in, cos, sin)


def _conv_kernel(main_ref, prev_ref, next_ref, w_ref, b_ref, g_ref, o_ref, win_ref, sh_ref, y_ref):
    i = pl.program_id(1)
    n = pl.num_programs(1)
    prev = prev_ref[0]
    nxt = next_ref[0]
    win_ref[0:HALO, :] = jnp.where(i > 0, prev, jnp.zeros_like(prev))
    win_ref[HALO:HALO + TM, :] = main_ref[0]
    win_ref[HALO + TM:2 * HALO + TM, :] = jnp.where(i < n - 1, nxt, jnp.zeros_like(nxt))
    rows = TM + 2 * HALO
    lead = HALO - (CONV_WIDTH - 1) // 2
    for cb in range(CONV_CH // 128):
        lanes = slice(cb * 128, (cb + 1) * 128)
        wblk = win_ref[:, lanes]
        for r in range(1, 8):
            sh_ref[r - 1] = pltpu.roll(wblk, rows - r, 0)
        for rb in range(TM // CONV_ROWS):
            base = rb * CONV_ROWS
            acc = jnp.broadcast_to(b_ref[:, lanes], (CONV_ROWS, 128))
            for j in range(CONV_WIDTH):
                q, r = divmod(j + lead, 8)
                start = base + 8 * q
                if r == 0:
                    src = win_ref[start:start + CONV_ROWS, lanes]
                else:
                    src = sh_ref[r - 1, start:start + CONV_ROWS, :]
                acc = acc + src * w_ref[j:j + 1, lanes]
            y_ref[base:base + CONV_ROWS, lanes] = acc
    y = _rms(y_ref[...], g_ref[...])
    o_ref[0] = (y * jax.nn.sigmoid(y)).astype(BF16)


def _halo_specs(width, seq_len, tile=TM):
    per_tile = tile // HALO
    last = seq_len // HALO - 1

    def prev_map(bi, i):
        return (bi, jnp.maximum(i * per_tile - 1, 0), 0)

    def next_map(bi, i):
        return (bi, jnp.minimum((i + 1) * per_tile, last), 0)

    return (pl.BlockSpec((1, HALO, width), prev_map), pl.BlockSpec((1, HALO, width), next_map))


def _conv_mixer(glu, w, bias, g):
    b, s, _ = glu.shape
    prev_spec, next_spec = _halo_specs(CONV_CH, s)
    return pl.pallas_call(
        _conv_kernel,
        grid=(b, s // TM),
        in_specs=[
            pl.BlockSpec((1, TM, CONV_CH), lambda bi, i: (bi, i, 0)),
            prev_spec,
            next_spec,
            _const_spec((CONV_WIDTH + 1, CONV_CH)),
            _const_spec((1, CONV_CH)),
            _const_spec((1, CONV_CH)),
        ],
        out_specs=pl.BlockSpec((1, TM, CONV_CH), lambda bi, i: (bi, i, 0)),
        out_shape=jax.ShapeDtypeStruct((b, s, CONV_CH), BF16),
        scratch_shapes=[pltpu.VMEM((TM + 2 * HALO, CONV_CH), F32),
                        pltpu.VMEM((7, TM + 2 * HALO, 128), F32), pltpu.VMEM((TM, CONV_CH), F32)],
        compiler_params=_params(2),
        name="conv_mixer",
    )(glu, glu, glu, w, bias, g)


def _attn_kernel(qt_ref, k_ref, vt_ref, lq1_ref, lk1_ref, lq2_ref, lk2_ref, g_ref, o_ref,
                 s_ref, p_ref, acc_ref, *, n_chunks, n_qtiles):
    n_items = n_chunks * n_qtiles
    lg = n_chunks.bit_length() - 1
    tk = s_ref.shape[0]
    zeros = jnp.zeros((DIFF_DQ, TQ), BF16)
    lam = (jnp.exp(jnp.sum(lq1_ref[...] * lk1_ref[...], axis=-1, keepdims=True))
           - jnp.exp(jnp.sum(lq2_ref[...] * lk2_ref[...], axis=-1, keepdims=True)) + LAM_INIT)

    def stage1(t, m_prev):
        qi = t >> lg
        c = t & (n_chunks - 1)
        qt = qt_ref[0, 0, qi]
        qq = jnp.concatenate(
            [jnp.concatenate([qt[:DIFF_DQ], zeros], axis=0),
             jnp.concatenate([zeros, qt[DIFF_DQ:]], axis=0)], axis=1)
        off = pl.multiple_of(c * tk, tk)
        s = jnp.dot(k_ref[0, pl.ds(off, tk), :], qq, preferred_element_type=F32)
        s_ref[...] = s
        m_prev = jnp.where(c == 0, NEG_BIG, m_prev)
        m_new = jnp.maximum(m_prev, jnp.max(s, axis=0, keepdims=True))
        return m_new, jnp.exp2(m_prev - m_new)

    def stage2(m):
        p_ref[...] = jnp.exp2(s_ref[...] - m).astype(BF16)

    def stage3(t, a):
        c = t & (n_chunks - 1)
        pv = acc_ref[...] * a
        for i in range(tk // TM):
            pv = pv + jnp.dot(vt_ref[0, 0, (tk // TM) * c + i], p_ref[i * TM:(i + 1) * TM, :],
                              preferred_element_type=F32)
        acc_ref[...] = pv

    def finalize(qi):
        acc = acc_ref[...]
        o = acc[:DIFF_DV, :] * (1.0 / acc[DIFF_DV:DIFF_DV + 1, :])
        d = o[:, :TQ] - lam * o[:, TQ:]
        ms = jnp.mean(d * d, axis=0, keepdims=True)
        y = d * lax.rsqrt(ms + EPS) * g_ref[...] * (1.0 - LAM_INIT)
        o_ref[0, pl.ds(pl.multiple_of(qi * TQ, TQ), TQ), :] = y.T.astype(BF16)

    acc_ref[...] = jnp.zeros_like(acc_ref)
    m0, a0 = stage1(0, jnp.full((1, 2 * TQ), NEG_BIG, F32))
    stage2(m0)
    m1, a1 = stage1(1, m0)

    def step(t, carry, may_finalize):
        m_prev, a_prev1, a_prev2 = carry
        stage3(t - 2, a_prev2)
        stage2(m_prev)
        m_t, a_t = stage1(t, m_prev)
        if may_finalize:
            @pl.when(((t - 2) & (n_chunks - 1)) == n_chunks - 1)
            def _():
                finalize((t - 2) >> lg)

        return m_t, a_t, a_prev1

    def pair(j, carry):
        t = 2 * j + 2
        return step(t + 1, step(t, carry, False), True)

    m_last, a_last1, a_last2 = lax.fori_loop(0, (n_items - 2) // 2, pair, (m1, a1, a0))
    stage3(n_items - 2, a_last2)
    stage2(m_last)
    stage3(n_items - 1, a_last1)
    finalize(n_qtiles - 1)


def _diff_attention(qt, k, vt, lq1, lk1, lq2, lk2, g_col):
    b, s, _ = k.shape
    tk = min(TK, s // 2)
    n_chunks = s // tk
    n_qtiles = s // TQ
    assert n_chunks & (n_chunks - 1) == 0 and n_chunks >= 2 and tk % TM == 0
    return pl.pallas_call(
        functools.partial(_attn_kernel, n_chunks=n_chunks, n_qtiles=n_qtiles),
        grid=(b, DIFF_HEADS),
        in_specs=[
            pl.BlockSpec((1, 1, n_qtiles, 2 * DIFF_DQ, TQ), lambda bi, h: (bi, h, 0, 0, 0)),
            pl.BlockSpec((1, s, 2 * DIFF_DQ), lambda bi, h: (bi, 0, h)),
            pl.BlockSpec((1, 1, s // TM, V_ROWS, TM), lambda bi, h: (bi, h, 0, 0, 0)),
            _const_spec((1, DIFF_DQ)),
            _const_spec((1, DIFF_DQ)),
            _const_spec((1, DIFF_DQ)),
            _const_spec((1, DIFF_DQ)),
            _const_spec((DIFF_DV, 1)),
        ],
        out_specs=pl.BlockSpec((1, s, DIFF_DV), lambda bi, h: (bi, 0, h)),
        out_shape=jax.ShapeDtypeStruct((b, s, DIFF_WIDTH), BF16),
        scratch_shapes=[pltpu.VMEM((tk, 2 * TQ), F32), pltpu.VMEM((tk, 2 * TQ), BF16),
                        pltpu.VMEM((V_ROWS, 2 * TQ), F32)],
        compiler_params=_params(2),
        name="diff_attention",
    )(qt, k, vt, lq1, lk1, lq2, lk2, g_col)


def _memkv_kernel(m_ref, g_ref, w_ref, k_ref, v_ref):
    h = _rms(m_ref[0], g_ref[...]).astype(BF16)
    kv = jnp.dot(h, w_ref[...], preferred_element_type=F32)
    k_ref[0] = kv[:, :D_MODEL].astype(BF16)
    v_ref[0] = kv[:, D_MODEL:].astype(BF16)


def _mem_kv(mem, g, w_mkv):
    b = mem.shape[0]
    blk = pl.BlockSpec((1, N_MEM, D_MODEL), lambda bi: (bi, 0, 0))
    return pl.pallas_call(
        _memkv_kernel,
        grid=(b,),
        in_specs=[blk, _const_spec((1, D_MODEL)), _const_spec((D_MODEL, 2 * D_MODEL))],
        out_specs=[blk, blk],
        out_shape=[jax.ShapeDtypeStruct((b, N_MEM, D_MODEL), BF16)] * 2,
        compiler_params=_params(1),
        name="mem_kv",
    )(mem, g, w_mkv)


def _cross_kernel(x_ref, a_ref, b_ref, wout_ref, gc_ref, wq_ref, km_ref, vm_ref, wo_ref, gf_ref,
                  x2_ref, hf_ref):
    x1 = (x_ref[0]
          + jnp.dot(a_ref[0], wout_ref[:CONV_CH, :], preferred_element_type=F32)
          + jnp.dot(b_ref[0], wout_ref[CONV_CH:, :], preferred_element_type=F32))
    hq = _rms(x1, gc_ref[...]).astype(BF16)
    q = (jnp.dot(hq, wq_ref[...], preferred_element_type=F32) * (MEM_HD ** -0.5)).astype(BF16)
    heads = []
    for h in range(MEM_HEADS):
        sl = slice(h * MEM_HD, (h + 1) * MEM_HD)
        s = lax.dot_general(q[:, sl], km_ref[0, :, sl], (((1,), (1,)), ((), ())),
                            preferred_element_type=F32)
        p = jnp.exp(s - jnp.max(s, axis=-1, keepdims=True))
        l = jnp.sum(p, axis=-1, keepdims=True)
        o = jnp.dot(p.astype(BF16), vm_ref[0, :, sl], preferred_element_type=F32)
        heads.append((o * (1.0 / l)).astype(BF16))
    o_all = jnp.concatenate(heads, axis=-1)
    x2 = x1 + jnp.dot(o_all, wo_ref[...], preferred_element_type=F32)
    x2_ref[0] = x2
    hf_ref[0] = _rms(x2, gf_ref[...]).astype(BF16)


def _cross(x, a, bb, w_out, g_cross, w_mq, k_mem, v_mem, w_mo, g_ffn):
    b, s, _ = x.shape
    tile = lambda width: pl.BlockSpec((1, TC, width), lambda bi, i: (bi, i, 0))
    mem_blk = pl.BlockSpec((1, N_MEM, D_MODEL), lambda bi, i: (bi, 0, 0))
    return pl.pallas_call(
        _cross_kernel,
        grid=(b, s // TC),
        in_specs=[
            tile(D_MODEL), tile(CONV_CH), tile(DIFF_WIDTH),
            _const_spec((D_MODEL, D_MODEL)), _const_spec((1, D_MODEL)),
            _const_spec((D_MODEL, D_MODEL)), mem_blk, mem_blk,
            _const_spec((D_MODEL, D_MODEL)), _const_spec((1, D_MODEL)),
        ],
        out_specs=[tile(D_MODEL), tile(D_MODEL)],
        out_shape=[jax.ShapeDtypeStruct((b, s, D_MODEL), F32),
                   jax.ShapeDtypeStruct((b, s, D_MODEL), BF16)],
        compiler_params=_params(2),
        name="cross_attention",
    )(x, a, bb, w_out, g_cross, w_mq, k_mem, v_mem, w_mo, g_ffn)


def _ffn_kernel(h_ref, hprev_ref, hnext_ref, x2_ref, wup_ref, dw_ref, db_ref, wdown_ref, gfin_ref,
                y_ref, gate_ref):
    i = pl.program_id(1)
    n = pl.num_programs(1)
    prev = hprev_ref[0]
    nxt = hnext_ref[0]
    hwin = jnp.concatenate(
        [jnp.where(i > 0, prev, jnp.zeros_like(prev)), h_ref[0],
         jnp.where(i < n - 1, nxt, jnp.zeros_like(nxt))], axis=0)

    def conv3(col0):
        cols = slice(col0, col0 + FF_CHUNK)
        u = jnp.dot(hwin, wup_ref[:, cols], preferred_element_type=F32)
        return (pltpu.roll(u, 1, 0)[HALO:HALO + TF] * dw_ref[0:1, cols]
                + u[HALO:HALO + TF] * dw_ref[1:2, cols]
                + pltpu.roll(u, TF + 2 * HALO - 1, 0)[HALO:HALO + TF] * dw_ref[2:3, cols]
                + db_ref[:, cols])

    for c in range(D_FF // FF_CHUNK):
        val = conv3(c * FF_CHUNK)
        gate = conv3(D_FF + c * FF_CHUNK)
        gate_ref[:, c * FF_CHUNK:(c + 1) * FF_CHUNK] = (gate * jax.nn.sigmoid(gate) * val).astype(BF16)
    x3 = x2_ref[0] + jnp.dot(gate_ref[...], wdown_ref[...], preferred_element_type=F32)
    y_ref[0] = _rms(x3, gfin_ref[...])


def _ffn(hf, x2, w_up, dw_w, dw_b, w_down, g_final):
    b, s, _ = x2.shape
    tile = pl.BlockSpec((1, TF, D_MODEL), lambda bi, i: (bi, i, 0))
    prev_spec, next_spec = _halo_specs(D_MODEL, s, TF)
    return pl.pallas_call(
        _ffn_kernel,
        grid=(b, s // TF),
        in_specs=[
            tile, prev_spec, next_spec, tile,
            _const_spec((D_MODEL, 2 * D_FF)), _const_spec((8, 2 * D_FF)), _const_spec((1, 2 * D_FF)),
            _const_spec((D_FF, D_MODEL)), _const_spec((1, D_MODEL)),
        ],
        out_specs=tile,
        out_shape=jax.ShapeDtypeStruct((b, s, D_MODEL), F32),
        scratch_shapes=[pltpu.VMEM((TF, D_FF), BF16)],
        compiler_params=_params(2),
        name="conv_ffn",
    )(hf, hf, hf, x2, w_up, dw_w, dw_b, w_down, g_final)


def _rope_tables(s):
    inv = ROPE_THETA ** (-jnp.arange(0, DIFF_DQ, 2, dtype=F32) / DIFF_DQ)
    ang = jnp.arange(s, dtype=F32)[:, None] * inv[None, :]
    c, sn = jnp.cos(ang), jnp.sin(ang)
    return jnp.concatenate([c, c, c, c], axis=1), jnp.concatenate([-sn, sn, -sn, sn], axis=1)


def _encoder(x, mem, p):
    cos, sin = _rope_tables(x.shape[1])
    glu, qt, k, vt = _inproj(x, p["norm_mix"], p["w_in"], cos, sin)
    a = _conv_mixer(glu, p["conv_dw_w"], p["conv_dw_b"], p["conv_norm"])
    bb = _diff_attention(qt, k, vt, p["lq1"], p["lk1"], p["lq2"], p["lk2"], p["diff_subln"])
    k_mem, v_mem = _mem_kv(mem, p["norm_mem"], p["w_mkv"])
    x2, hf = _cross(x, a, bb, p["w_out"], p["norm_cross"], p["w_mq"], k_mem, v_mem, p["w_mo"],
                    p["norm_ffn"])
    return _ffn(hf, x2, p["w_up"], p["ffn_dw_w"], p["ffn_dw_b"], p["w_down"], p["norm_final"])


def kernel(x_prompt, x_sample, mem_prompt, mem_sample, norm_mix, w_in, conv_dw_w, conv_dw_b, conv_norm, lambda_q1, lambda_k1, lambda_q2, lambda_k2, diff_subln, w_out, norm_cross, norm_mem, w_mq, w_mkv, w_mo, norm_ffn, w_up, ffn_dw_w, ffn_dw_b, w_down, norm_final):
    row = lambda v: v.reshape(1, -1).astype(F32)
    p = {
        "norm_mix": row(norm_mix[0]),
        "w_in": w_in[0].astype(BF16),
        "conv_dw_w": jnp.pad(conv_dw_w[0], ((0, 1), (0, 0))),
        "conv_dw_b": row(conv_dw_b[0]),
        "conv_norm": row(conv_norm[0]),
        "lq1": row(lambda_q1[0]), "lk1": row(lambda_k1[0]),
        "lq2": row(lambda_q2[0]), "lk2": row(lambda_k2[0]),
        "diff_subln": diff_subln[0].reshape(-1, 1).astype(F32),
        "w_out": w_out[0].astype(BF16),
        "norm_cross": row(norm_cross[0]),
        "norm_mem": row(norm_mem[0]),
        "w_mq": w_mq[0].astype(BF16),
        "w_mkv": w_mkv[0].astype(BF16),
        "w_mo": w_mo[0].astype(BF16),
        "norm_ffn": row(norm_ffn[0]),
        "w_up": w_up[0].astype(BF16),
        "ffn_dw_w": jnp.pad(ffn_dw_w[0], ((0, 5), (0, 0))),
        "ffn_dw_b": row(ffn_dw_b[0]),
        "w_down": w_down[0].astype(BF16),
        "norm_final": row(norm_final),
    }
    return (_encoder(x_prompt, mem_prompt, p), _encoder(x_sample, mem_sample, p))
```

```python
import functools
import math

import jax
import jax.numpy as jnp
from jax import lax
from jax.experimental import pallas as pl
from jax.experimental.pallas import tpu as pltpu

F32 = jnp.float32
BF16 = jnp.bfloat16

D_MODEL = 1024
N_MEM = 256
CONV_CH = 512
CONV_WIDTH = 31
DIFF_HEADS = 4
DIFF_DQ = 64
DIFF_DV = 128
DIFF_WIDTH = DIFF_HEADS * DIFF_DV
IN_COLS = 2 * CONV_CH + 3 * DIFF_WIDTH
MEM_HEADS = 4
MEM_HD = D_MODEL // MEM_HEADS
D_FF = 2816
ROPE_THETA = 10000.0
EPS = 1e-6
LAM_INIT = 0.8 - 0.6 * math.exp(-0.3 * 0)
Q_SCALE = DIFF_DQ ** -0.5 * math.log2(math.e)

TM = 512
TC = 1024
TF = 1024
TQ = 512
TK = 2048
V_ROWS = DIFF_DV + 16
HALO = 16
CONV_ROWS = 64
FF_CHUNK = 256
VMEM_LIMIT = 56 * 1024 * 1024
NEG_BIG = -1e30


def _rms(x, g):
    ms = jnp.mean(x * x, axis=-1, keepdims=True)
    return x * lax.rsqrt(ms + EPS) * g


def _params(n_axes):
    return pltpu.CompilerParams(dimension_semantics=("arbitrary",) * n_axes,
                                vmem_limit_bytes=VMEM_LIMIT)


def _const_spec(shape):
    return pl.BlockSpec(shape, lambda *_: (0,) * len(shape), pipeline_mode=pl.Buffered(1))


def _halo_specs(width, seq_len, tile=TM):
    per_tile = tile // HALO
    last = seq_len // HALO - 1

    def prev_map(bi, i):
        return (bi, jnp.maximum(i * per_tile - 1, 0), 0)

    def next_map(bi, i):
        return (bi, jnp.minimum((i + 1) * per_tile, last), 0)

    return (pl.BlockSpec((1, HALO, width), prev_map), pl.BlockSpec((1, HALO, width), next_map))


def _inproj_kernel(x_ref, xprev_ref, xnext_ref, g_ref, w_ref, cos_ref, sin_ref, cw_ref, cb_ref, cg_ref,
                   a_ref, qt_ref, k_ref, vt_ref, win_ref, sh_ref, y_ref):
    i = pl.program_id(1)
    n = pl.num_programs(1)
    rows = TM + 2 * HALO
    g = g_ref[...]
    h = _rms(x_ref[0], g).astype(BF16)
    hwin = jnp.concatenate([_rms(xprev_ref[0], g).astype(BF16), h, _rms(xnext_ref[0], g).astype(BF16)],
                           axis=0)

    zg = jnp.dot(hwin, w_ref[:, :2 * CONV_CH], preferred_element_type=F32)
    glu = zg[:, :CONV_CH] * jax.nn.sigmoid(zg[:, CONV_CH:])
    head = glu[:HALO]
    tail = glu[HALO + TM:]
    win_ref[0:HALO, :] = jnp.where(i > 0, head, jnp.zeros_like(head))
    win_ref[HALO:HALO + TM, :] = glu[HALO:HALO + TM]
    win_ref[HALO + TM:rows, :] = jnp.where(i < n - 1, tail, jnp.zeros_like(tail))

    z = jnp.dot(h, w_ref[:, 2 * CONV_CH:], preferred_element_type=F32)
    cos = cos_ref[...]
    sin = sin_ref[...]
    lane = lax.broadcasted_iota(jnp.int32, (TM, 128), 1)
    first_half = (lane % DIFF_DQ) < (DIFF_DQ // 2)

    def rope(t):
        partner = jnp.where(first_half, pltpu.roll(t, 96, 1), pltpu.roll(t, 32, 1))
        return t * cos + partner * sin

    k_off = DIFF_WIDTH
    v_off = 2 * DIFF_WIDTH
    for j in range(DIFF_HEADS):
        q = rope(z[:, 128 * j:128 * (j + 1)]) * Q_SCALE
        qt_ref[0, j, 0] = q.T.astype(BF16)
        k = rope(z[:, k_off + 128 * j:k_off + 128 * (j + 1)])
        k_ref[0, :, 128 * j:128 * (j + 1)] = k.astype(BF16)
        v = z[:, v_off + 128 * j:v_off + 128 * (j + 1)]
        vt_ref[0, j, 0, 0:DIFF_DV, :] = v.T.astype(BF16)
        pad_row = lax.broadcasted_iota(jnp.int32, (V_ROWS - DIFF_DV, TM), 0)
        vt_ref[0, j, 0, DIFF_DV:V_ROWS, :] = jnp.where(pad_row == 0, 1.0, 0.0).astype(BF16)

    lead = HALO - (CONV_WIDTH - 1) // 2
    for cb in range(CONV_CH // 128):
        lanes = slice(cb * 128, (cb + 1) * 128)
        wblk = win_ref[:, lanes]
        for r in range(1, 8):
            sh_ref[r - 1] = pltpu.roll(wblk, rows - r, 0)
        for rb in range(TM // CONV_ROWS):
            base = rb * CONV_ROWS
            acc = jnp.broadcast_to(cb_ref[:, lanes], (CONV_ROWS, 128))
            for j in range(CONV_WIDTH):
                q, r = divmod(j + lead, 8)
                start = base + 8 * q
                if r == 0:
                    src = win_ref[start:start + CONV_ROWS, lanes]
                else:
                    src = sh_ref[r - 1, start:start + CONV_ROWS, :]
                acc = acc + src * cw_ref[j:j + 1, lanes]
            y_ref[base:base + CONV_ROWS, lanes] = acc
    y = _rms(y_ref[...], cg_ref[...])
    a_ref[0] = (y * jax.nn.sigmoid(y)).astype(BF16)


def _inproj(x, g, w_in, cos, sin, conv_w, conv_b, conv_g):
    b, s, _ = x.shape
    nt = s // TM
    prev_spec, next_spec = _halo_specs(D_MODEL, s)
    return pl.pallas_call(
        _inproj_kernel,
        grid=(b, nt),
        in_specs=[
            pl.BlockSpec((1, TM, D_MODEL), lambda bi, i: (bi, i, 0)),
            prev_spec,
            next_spec,
            _const_spec((1, D_MODEL)),
            _const_spec((D_MODEL, IN_COLS)),
            pl.BlockSpec((TM, 128), lambda bi, i: (i, 0)),
            pl.BlockSpec((TM, 128), lambda bi, i: (i, 0)),
            _const_spec((CONV_WIDTH + 1, CONV_CH)),
            _const_spec((1, CONV_CH)),
            _const_spec((1, CONV_CH)),
        ],
        out_specs=[
            pl.BlockSpec((1, TM, CONV_CH), lambda bi, i: (bi, i, 0)),
            pl.BlockSpec((1, DIFF_HEADS, 1, 2 * DIFF_DQ, TM), lambda bi, i: (bi, 0, i, 0, 0)),
            pl.BlockSpec((1, TM, DIFF_WIDTH), lambda bi, i: (bi, i, 0)),
            pl.BlockSpec((1, DIFF_HEADS, 1, V_ROWS, TM), lambda bi, i: (bi, 0, i, 0, 0)),
        ],
        out_shape=[
            jax.ShapeDtypeStruct((b, s, CONV_CH), BF16),
            jax.ShapeDtypeStruct((b, DIFF_HEADS, nt, 2 * DIFF_DQ, TM), BF16),
            jax.ShapeDtypeStruct((b, s, DIFF_WIDTH), BF16),
            jax.ShapeDtypeStruct((b, DIFF_HEADS, nt, V_ROWS, TM), BF16),
        ],
        scratch_shapes=[pltpu.VMEM((TM + 2 * HALO, CONV_CH), F32),
                        pltpu.VMEM((7, TM + 2 * HALO, 128), F32), pltpu.VMEM((TM, CONV_CH), F32)],
        compiler_params=_params(2),
        name="inproj",
    )(x, x, x, g, w_in, cos, sin, conv_w, conv_b, conv_g)


def _attn_kernel(qt_ref, k_ref, vt_ref, lq1_ref, lk1_ref, lq2_ref, lk2_ref, g_ref, o_ref,
                 s_ref, p_ref, acc_ref, *, n_chunks, n_qtiles):
    n_items = n_chunks * n_qtiles
    lg = n_chunks.bit_length() - 1
    tk = s_ref.shape[0]
    zeros = jnp.zeros((DIFF_DQ, TQ), BF16)
    lam = (jnp.exp(jnp.sum(lq1_ref[...] * lk1_ref[...], axis=-1, keepdims=True))
           - jnp.exp(jnp.sum(lq2_ref[...] * lk2_ref[...], axis=-1, keepdims=True)) + LAM_INIT)

    def stage1(t, m_prev):
        qi = t >> lg
        c = t & (n_chunks - 1)
        qt = qt_ref[0, 0, qi]
        qq = jnp.concatenate(
            [jnp.concatenate([qt[:DIFF_DQ], zeros], axis=0),
             jnp.concatenate([zeros, qt[DIFF_DQ:]], axis=0)], axis=1)
        off = pl.multiple_of(c * tk, tk)
        s = jnp.dot(k_ref[0, pl.ds(off, tk), :], qq, preferred_element_type=F32)
        s_ref[...] = s
        m_prev = jnp.where(c == 0, NEG_BIG, m_prev)
        m_new = jnp.maximum(m_prev, jnp.max(s, axis=0, keepdims=True))
        return m_new, jnp.exp2(m_prev - m_new)

    def stage2(m):
        p_ref[...] = jnp.exp2(s_ref[...] - m).astype(BF16)

    def stage3(t, a):
        c = t & (n_chunks - 1)
        pv = acc_ref[...] * a
        for i in range(tk // TM):
            pv = pv + jnp.dot(vt_ref[0, 0, (tk // TM) * c + i], p_ref[i * TM:(i + 1) * TM, :],
                              preferred_element_type=F32)
        acc_ref[...] = pv

    def finalize(qi):
        acc = acc_ref[...]
        o = acc[:DIFF_DV, :] * (1.0 / acc[DIFF_DV:DIFF_DV + 1, :])
        d = o[:, :TQ] - lam * o[:, TQ:]
        ms = jnp.mean(d * d, axis=0, keepdims=True)
        y = d * lax.rsqrt(ms + EPS) * g_ref[...] * (1.0 - LAM_INIT)
        o_ref[0, pl.ds(pl.multiple_of(qi * TQ, TQ), TQ), :] = y.T.astype(BF16)

    acc_ref[...] = jnp.zeros_like(acc_ref)
    m0, a0 = stage1(0, jnp.full((1, 2 * TQ), NEG_BIG, F32))
    stage2(m0)
    m1, a1 = stage1(1, m0)

    def step(t, carry, may_finalize):
        m_prev, a_prev1, a_prev2 = carry
        stage3(t - 2, a_prev2)
        stage2(m_prev)
        m_t, a_t = stage1(t, m_prev)
        if may_finalize:
            @pl.when(((t - 2) & (n_chunks - 1)) == n_chunks - 1)
            def _():
                finalize((t - 2) >> lg)

        return m_t, a_t, a_prev1

    def pair(j, carry):
        t = 2 * j + 2
        return step(t + 1, step(t, carry, False), True)

    m_last, a_last1, a_last2 = lax.fori_loop(0, (n_items - 2) // 2, pair, (m1, a1, a0))
    stage3(n_items - 2, a_last2)
    stage2(m_last)
    stage3(n_items - 1, a_last1)
    finalize(n_qtiles - 1)


def _diff_attention(qt, k, vt, lq1, lk1, lq2, lk2, g_col):
    b, s, _ = k.shape
    tk = min(TK, s // 2)
    n_chunks = s // tk
    n_qtiles = s // TQ
    assert n_chunks & (n_chunks - 1) == 0 and n_chunks >= 2 and tk % TM == 0
    return pl.pallas_call(
        functools.partial(_attn_kernel, n_chunks=n_chunks, n_qtiles=n_qtiles),
        grid=(b, DIFF_HEADS),
        in_specs=[
            pl.BlockSpec((1, 1, n_qtiles, 2 * DIFF_DQ, TQ), lambda bi, h: (bi, h, 0, 0, 0)),
            pl.BlockSpec((1, s, 2 * DIFF_DQ), lambda bi, h: (bi, 0, h)),
            pl.BlockSpec((1, 1, s // TM, V_ROWS, TM), lambda bi, h: (bi, h, 0, 0, 0)),
            _const_spec((1, DIFF_DQ)),
            _const_spec((1, DIFF_DQ)),
            _const_spec((1, DIFF_DQ)),
            _const_spec((1, DIFF_DQ)),
            _const_spec((DIFF_DV, 1)),
        ],
        out_specs=pl.BlockSpec((1, s, DIFF_DV), lambda bi, h: (bi, 0, h)),
        out_shape=jax.ShapeDtypeStruct((b, s, DIFF_WIDTH), BF16),
        scratch_shapes=[pltpu.VMEM((tk, 2 * TQ), F32), pltpu.VMEM((tk, 2 * TQ), BF16),
                        pltpu.VMEM((V_ROWS, 2 * TQ), F32)],
        compiler_params=_params(2),
        name="diff_attention",
    )(qt, k, vt, lq1, lk1, lq2, lk2, g_col)


def _memkv_kernel(m_ref, g_ref, w_ref, k_ref, v_ref):
    h = _rms(m_ref[0], g_ref[...]).astype(BF16)
    kv = jnp.dot(h, w_ref[...], preferred_element_type=F32)
    k_ref[0] = kv[:, :D_MODEL].astype(BF16)
    v_ref[0] = kv[:, D_MODEL:].astype(BF16)


def _mem_kv(mem, g, w_mkv):
    b = mem.shape[0]
    blk = pl.BlockSpec((1, N_MEM, D_MODEL), lambda bi: (bi, 0, 0))
    return pl.pallas_call(
        _memkv_kernel,
        grid=(b,),
        in_specs=[blk, _const_spec((1, D_MODEL)), _const_spec((D_MODEL, 2 * D_MODEL))],
        out_specs=[blk, blk],
        out_shape=[jax.ShapeDtypeStruct((b, N_MEM, D_MODEL), BF16)] * 2,
        compiler_params=_params(1),
        name="mem_kv",
    )(mem, g, w_mkv)


def _cross_kernel(x_ref, a_ref, b_ref, wout_ref, gc_ref, wq_ref, km_ref, vm_ref, wo_ref, gf_ref,
                  x2_ref, hf_ref):
    x1 = (x_ref[0]
          + jnp.dot(a_ref[0], wout_ref[:CONV_CH, :], preferred_element_type=F32)
          + jnp.dot(b_ref[0], wout_ref[CONV_CH:, :], preferred_element_type=F32))
    hq = _rms(x1, gc_ref[...]).astype(BF16)
    q = (jnp.dot(hq, wq_ref[...], preferred_element_type=F32) * (MEM_HD ** -0.5)).astype(BF16)
    heads = []
    for h in range(MEM_HEADS):
        sl = slice(h * MEM_HD, (h + 1) * MEM_HD)
        s = lax.dot_general(q[:, sl], km_ref[0, :, sl], (((1,), (1,)), ((), ())),
                            preferred_element_type=F32)
        p = jnp.exp(s - jnp.max(s, axis=-1, keepdims=True))
        l = jnp.sum(p, axis=-1, keepdims=True)
        o = jnp.dot(p.astype(BF16), vm_ref[0, :, sl], preferred_element_type=F32)
        heads.append((o * (1.0 / l)).astype(BF16))
    o_all = jnp.concatenate(heads, axis=-1)
    x2 = x1 + jnp.dot(o_all, wo_ref[...], preferred_element_type=F32)
    x2_ref[0] = x2
    hf_ref[0] = _rms(x2, gf_ref[...]).astype(BF16)


def _cross(x, a, bb, w_out, g_cross, w_mq, k_mem, v_mem, w_mo, g_ffn):
    b, s, _ = x.shape
    tile = lambda width: pl.BlockSpec((1, TC, width), lambda bi, i: (bi, i, 0))
    mem_blk = pl.BlockSpec((1, N_MEM, D_MODEL), lambda bi, i: (bi, 0, 0))
    return pl.pallas_call(
        _cross_kernel,
        grid=(b, s // TC),
        in_specs=[
            tile(D_MODEL), tile(CONV_CH), tile(DIFF_WIDTH),
            _const_spec((D_MODEL, D_MODEL)), _const_spec((1, D_MODEL)),
            _const_spec((D_MODEL, D_MODEL)), mem_blk, mem_blk,
            _const_spec((D_MODEL, D_MODEL)), _const_spec((1, D_MODEL)),
        ],
        out_specs=[tile(D_MODEL), tile(D_MODEL)],
        out_shape=[jax.ShapeDtypeStruct((b, s, D_MODEL), F32),
                   jax.ShapeDtypeStruct((b, s, D_MODEL), BF16)],
        compiler_params=_params(2),
        name="cross_attention",
    )(x, a, bb, w_out, g_cross, w_mq, k_mem, v_mem, w_mo, g_ffn)


def _ffn_kernel(h_ref, hprev_ref, hnext_ref, x2_ref, wup_ref, dw_ref, db_ref, wdown_ref, gfin_ref,
                y_ref, gate_ref):
    i = pl.program_id(1)
    n = pl.num_programs(1)
    prev = hprev_ref[0]
    nxt = hnext_ref[0]
    hwin = jnp.concatenate(
        [jnp.where(i > 0, prev, jnp.zeros_like(prev)), h_ref[0],
         jnp.where(i < n - 1, nxt, jnp.zeros_like(nxt))], axis=0)

    def conv3(col0):
        cols = slice(col0, col0 + FF_CHUNK)
        u = jnp.dot(hwin, wup_ref[:, cols], preferred_element_type=F32)
        return (pltpu.roll(u, 1, 0)[HALO:HALO + TF] * dw_ref[0:1, cols]
                + u[HALO:HALO + TF] * dw_ref[1:2, cols]
                + pltpu.roll(u, TF + 2 * HALO - 1, 0)[HALO:HALO + TF] * dw_ref[2:3, cols]
                + db_ref[:, cols])

    for c in range(D_FF // FF_CHUNK):
        val = conv3(c * FF_CHUNK)
        gate = conv3(D_FF + c * FF_CHUNK)
        gate_ref[:, c * FF_CHUNK:(c + 1) * FF_CHUNK] = (gate * jax.nn.sigmoid(gate) * val).astype(BF16)
    x3 = x2_ref[0] + jnp.dot(gate_ref[...], wdown_ref[...], preferred_element_type=F32)
    y_ref[0] = _rms(x3, gfin_ref[...])


def _ffn(hf, x2, w_up, dw_w, dw_b, w_down, g_final):
    b, s, _ = x2.shape
    tile = pl.BlockSpec((1, TF, D_MODEL), lambda bi, i: (bi, i, 0))
    prev_spec, next_spec = _halo_specs(D_MODEL, s, TF)
    return pl.pallas_call(
        _ffn_kernel,
        grid=(b, s // TF),
        in_specs=[
            tile, prev_spec, next_spec, tile,
            _const_spec((D_MODEL, 2 * D_FF)), _const_spec((8, 2 * D_FF)), _const_spec((1, 2 * D_FF)),
            _const_spec((D_FF, D_MODEL)), _const_spec((1, D_MODEL)),
        ],
        out_specs=tile,
        out_shape=jax.ShapeDtypeStruct((b, s, D_MODEL), F32),
        scratch_shapes=[pltpu.VMEM((TF, D_FF), BF16)],
        compiler_params=_params(2),
        name="conv_ffn",
    )(hf, hf, hf, x2, w_up, dw_w, dw_b, w_down, g_final)


def _rope_tables(s):
    inv = ROPE_THETA ** (-jnp.arange(0, DIFF_DQ, 2, dtype=F32) / DIFF_DQ)
    ang = jnp.arange(s, dtype=F32)[:, None] * inv[None, :]
    c, sn = jnp.cos(ang), jnp.sin(ang)
    return jnp.concatenate([c, c, c, c], axis=1), jnp.concatenate([-sn, sn, -sn, sn], axis=1)


def _encoder(x, mem, p):
    cos, sin = _rope_tables(x.shape[1])
    a, qt, k, vt = _inproj(x, p["norm_mix"], p["w_in"], cos, sin,
                           p["conv_dw_w"], p["conv_dw_b"], p["conv_norm"])
    bb = _diff_attention(qt, k, vt, p["lq1"], p["lk1"], p["lq2"], p["lk2"], p["diff_subln"])
    k_mem, v_mem = _mem_kv(mem, p["norm_mem"], p["w_mkv"])
    x2, hf = _cross(x, a, bb, p["w_out"], p["norm_cross"], p["w_mq"], k_mem, v_mem, p["w_mo"],
                    p["norm_ffn"])
    return _ffn(hf, x2, p["w_up"], p["ffn_dw_w"], p["ffn_dw_b"], p["w_down"], p["norm_final"])


def kernel(x_prompt, x_sample, mem_prompt, mem_sample, norm_mix, w_in, conv_dw_w, conv_dw_b, conv_norm, lambda_q1, lambda_k1, lambda_q2, lambda_k2, diff_subln, w_out, norm_cross, norm_mem, w_mq, w_mkv, w_mo, norm_ffn, w_up, ffn_dw_w, ffn_dw_b, w_down, norm_final):
    row = lambda v: v.reshape(1, -1).astype(F32)
    p = {
        "norm_mix": row(norm_mix[0]),
        "w_in": w_in[0].astype(BF16),
        "conv_dw_w": jnp.pad(conv_dw_w[0], ((0, 1), (0, 0))),
        "conv_dw_b": row(conv_dw_b[0]),
        "conv_norm": row(conv_norm[0]),
        "lq1": row(lambda_q1[0]), "lk1": row(lambda_k1[0]),
        "lq2": row(lambda_q2[0]), "lk2": row(lambda_k2[0]),
        "diff_subln": diff_subln[0].reshape(-1, 1).astype(F32),
        "w_out": w_out[0].astype(BF16),
        "norm_cross": row(norm_cross[0]),
        "norm_mem": row(norm_mem[0]),
        "w_mq": w_mq[0].astype(BF16),
        "w_mkv": w_mkv[0].astype(BF16),
        "w_mo": w_mo[0].astype(BF16),
        "norm_ffn": row(norm_ffn[0]),
        "w_up": w_up[0].astype(BF16),
        "ffn_dw_w": jnp.pad(ffn_dw_w[0], ((0, 5), (0, 0))),
        "ffn_dw_b": row(ffn_dw_b[0]),
        "w_down": w_down[0].astype(BF16),
        "norm_final": row(norm_final),
    }
    return (_encoder(x_prompt, mem_prompt, p), _encoder(x_sample, mem_sample, p))
```
